```python
import jax, jax.numpy as jnp
from jax import lax
import numpy as np

D_MODEL = 1024
BATCH = 16
SEQ = 2048
DEPTH = 1
DEC_BATCH = 32
DEC_SEQ = 4
PAST_LEN = 16384
PAGE_SIZE = 128

MOBA_HEADS = 8
MOBA_HEAD_DIM = 64
MOBA_WIDTH = MOBA_HEADS * MOBA_HEAD_DIM
MOBA_BLOCK = 256
MOBA_TOPK = 3
MOBA_QCHUNK = 16
ROPE_THETA = 500000.0
ROPE_DIM = MOBA_HEAD_DIM // 4
GLA_HEADS = 4
GLA_DK = 64
GLA_DV = 128
GLA_KEY_WIDTH = GLA_HEADS * GLA_DK
GLA_WIDTH = GLA_HEADS * GLA_DV
GLA_LOWRANK = 16
GLA_TAU = 16.0
GLA_CHUNK = 64
MIX_WIDTH = MOBA_WIDTH + GLA_WIDTH
IN_SPLITS = (MOBA_WIDTH, 2 * MOBA_WIDTH, 3 * MOBA_WIDTH,
             3 * MOBA_WIDTH + GLA_KEY_WIDTH,
             3 * MOBA_WIDTH + 2 * GLA_KEY_WIDTH,
             3 * MOBA_WIDTH + 2 * GLA_KEY_WIDTH + GLA_WIDTH,
             3 * MOBA_WIDTH + 2 * GLA_KEY_WIDTH + 2 * GLA_WIDTH)
IN_COLS = 3 * MOBA_WIDTH + 2 * GLA_KEY_WIDTH + 2 * GLA_WIDTH + GLA_LOWRANK
MEM_LEN = 256
MEM_HEADS = 4
MEM_HEAD_DIM = D_MODEL // MEM_HEADS
PEER_HEADS = 8
PEER_NKEYS = 128
PEER_EXPERTS = PEER_NKEYS * PEER_NKEYS
PEER_DQ = 256
PEER_TOPK = 16
PEER_TOKEN_CHUNK = 256
EPS = 1e-6
NEG = -1e30

kernel_name = 'moba_gla_peer_hybrid_step'


def rmsnorm(x, g):
    xf = x.astype(jnp.float32)
    y = xf * lax.rsqrt(jnp.mean(xf * xf, axis=-1, keepdims=True) + EPS)
    return (y * g.astype(jnp.float32)).astype(x.dtype)


def partial_rope(x, pos):
    half = ROPE_DIM // 2
    inv_freq = 1.0 / (ROPE_THETA ** (jnp.arange(half, dtype=jnp.float32) * (2.0 / ROPE_DIM)))
    ang = pos.astype(jnp.float32)[:, None] * inv_freq[None, :]
    cos = jnp.cos(ang)[:, None, :]
    sin = jnp.sin(ang)[:, None, :]
    xr = x[..., :ROPE_DIM].astype(jnp.float32)
    x1, x2 = xr[..., :half], xr[..., half:]
    rot = jnp.concatenate([x1 * cos - x2 * sin, x2 * cos + x1 * sin], axis=-1)
    return jnp.concatenate([rot.astype(x.dtype), x[..., ROPE_DIM:]], axis=-1)


def joint_softmax(parts):
    sizes = [p.shape[-1] for p in parts]
    p = jax.nn.softmax(jnp.concatenate(parts, axis=-1), axis=-1)
    return jnp.split(p, [int(s) for s in np.cumsum(sizes)[:-1]], axis=-1)


def select_blocks(gate, n_past):
    nb = gate.shape[-1]
    npad = max(nb, MOBA_TOPK)
    gate = jnp.pad(gate, [(0, 0)] * (gate.ndim - 1) + [(0, npad - nb)])
    gate = jnp.where(jnp.arange(npad) < n_past[:, None, None], gate, NEG)
    _, idx = lax.top_k(gate, MOBA_TOPK)
    idx = jnp.minimum(idx, max(nb - 1, 0))
    valid = jnp.arange(MOBA_TOPK) < n_past[:, None, None]
    return idx, valid


def mixer_inputs(h, pos, w_in, w_alpha_up, b_alpha):
    B, T, _ = h.shape
    z = h @ w_in
    qm, km, vm, qg, kg, vg, rg, ag = jnp.split(z, list(IN_SPLITS), axis=-1)
    qm = partial_rope(qm.reshape(B, T, MOBA_HEADS, MOBA_HEAD_DIM), pos)
    km = partial_rope(km.reshape(B, T, MOBA_HEADS, MOBA_HEAD_DIM), pos)
    vm = vm.reshape(B, T, MOBA_HEADS, MOBA_HEAD_DIM)
    qg = qg.reshape(B, T, GLA_HEADS, GLA_DK) * (GLA_DK ** -0.5)
    kg = kg.reshape(B, T, GLA_HEADS, GLA_DK)
    vg = vg.reshape(B, T, GLA_HEADS, GLA_DV)
    log_a = jax.nn.log_sigmoid((ag @ w_alpha_up + b_alpha).astype(jnp.float32)) / GLA_TAU
    return qm, km, vm, qg, kg, vg, rg, log_a.reshape(B, T, GLA_HEADS, GLA_DK)


def moba_prompt(q, k, v):
    B, S, H, dh = q.shape
    nblk = -(-S // MOBA_BLOCK)
    pad = nblk * MOBA_BLOCK - S
    kb = jnp.pad(k, ((0, 0), (0, pad), (0, 0), (0, 0))).reshape(B, nblk, MOBA_BLOCK, H, dh)
    vb = jnp.pad(v, ((0, 0), (0, pad), (0, 0), (0, 0))).reshape(B, nblk, MOBA_BLOCK, H, dh)
    kmean = jnp.mean(kb.astype(jnp.float32), axis=2)
    pos = jnp.arange(S)
    gate = jnp.einsum('bshd,bjhd->bshj', q.astype(jnp.float32), kmean)
    sel, valid = select_blocks(gate, pos // MOBA_BLOCK)
    nc = S // MOBA_QCHUNK
    qch = q.reshape(B, nc, MOBA_QCHUNK, H, dh).swapaxes(0, 1)
    selch = sel.reshape(B, nc, MOBA_QCHUNK, H, MOBA_TOPK).swapaxes(0, 1)
    valch = valid.reshape(nc, MOBA_QCHUNK, 1, MOBA_TOPK)
    bidx = jnp.arange(B)[:, None, None, None]
    hidx = jnp.arange(H)[None, None, :, None]
    lpos = jnp.arange(MOBA_BLOCK)
    scale = dh ** -0.5

    def attend(args):
        qc, selc, valc, c = args
        q0 = c * MOBA_QCHUNK
        qpos = q0 + jnp.arange(MOBA_QCHUNK)
        blk = q0 // MOBA_BLOCK
        ks = kb[bidx, selc, :, hidx, :]
        vs = vb[bidx, selc, :, hidx, :]
        kl = lax.dynamic_index_in_dim(kb, blk, axis=1, keepdims=False)
        vl = lax.dynamic_index_in_dim(vb, blk, axis=1, keepdims=False)
        s_sel = jnp.einsum('bqhd,bqhkld->bqhkl', qc, ks).astype(jnp.float32) * scale
        s_sel = jnp.where(valc[..., None], s_sel, NEG).reshape(B, MOBA_QCHUNK, H, MOBA_TOPK * MOBA_BLOCK)
        s_loc = jnp.einsum('bqhd,blhd->bqhl', qc, kl).astype(jnp.float32) * scale
        m_loc = (blk * MOBA_BLOCK + lpos)[None, :] <= qpos[:, None]
        s_loc = jnp.where(m_loc[:, None, :], s_loc, NEG)
        p_sel, p_loc = joint_softmax([s_sel, s_loc])
        p_sel = p_sel.reshape(B, MOBA_QCHUNK, H, MOBA_TOPK, MOBA_BLOCK).astype(v.dtype)
        return (jnp.einsum('bqhkl,bqhkld->bqhd', p_sel, vs)
                + jnp.einsum('bqhl,blhd->bqhd', p_loc.astype(v.dtype), vl))

    out = lax.map(attend, (qch, selch, valch, jnp.arange(nc)))
    return out.swapaxes(0, 1).reshape(B, S, H, dh)


def moba_sample(q, k_new, v_new, cache_k, cache_v, page_table):
    DB, T, H, dh = q.shape
    n_pages = page_table.shape[1]
    past = n_pages * PAGE_SIZE
    ppb = MOBA_BLOCK // PAGE_SIZE
    n_full = past // MOBA_BLOCK
    qpos = past + jnp.arange(T)
    qblk = qpos // MOBA_BLOCK
    scale = dh ** -0.5
    page_sum = jnp.sum(cache_k, axis=1, dtype=jnp.float32)
    kmean = page_sum[page_table[:, :n_full * ppb]].reshape(DB, n_full, ppb, H, dh).sum(axis=2) / MOBA_BLOCK
    gate = jnp.einsum('bthd,bjhd->bthj', q.astype(jnp.float32), kmean)
    sel, valid = select_blocks(gate, qblk)
    lp = jnp.minimum(sel[..., None] * ppb + jnp.arange(ppb), n_pages - 1)
    phys = page_table[jnp.arange(DB)[:, None, None, None, None], lp]
    hidx = jnp.arange(H)[None, None, :, None, None]
    ks = cache_k[phys, :, hidx, :].reshape(DB, T, H, MOBA_TOPK, MOBA_BLOCK, dh)
    vs = cache_v[phys, :, hidx, :].reshape(DB, T, H, MOBA_TOPK, MOBA_BLOCK, dh)
    own_lp = jnp.minimum(qblk[:, None] * ppb + jnp.arange(ppb), n_pages - 1)
    phys_own = page_table[:, own_lp]
    ko = cache_k[phys_own].reshape(DB, T, MOBA_BLOCK, H, dh)
    vo = cache_v[phys_own].reshape(DB, T, MOBA_BLOCK, H, dh)
    m_own = (qblk[:, None] * MOBA_BLOCK + jnp.arange(MOBA_BLOCK)[None, :]) < past
    tt = jnp.arange(T)
    m_new = (qblk[:, None] == qblk[None, :]) & (tt[None, :] <= tt[:, None])
    s_sel = jnp.einsum('bthd,bthkld->bthkl', q, ks).astype(jnp.float32) * scale
    s_sel = jnp.where(valid[..., None], s_sel, NEG).reshape(DB, T, H, MOBA_TOPK * MOBA_BLOCK)
    s_own = jnp.where(m_own[:, None, :], jnp.einsum('bthd,btlhd->bthl', q, ko).astype(jnp.float32) * scale, NEG)
    s_new = jnp.where(m_new[:, None, :], jnp.einsum('bthd,bshd->bths', q, k_new).astype(jnp.float32) * scale, NEG)
    p_sel, p_own, p_new = joint_softmax([s_sel, s_own, s_new])
    p_sel = p_sel.reshape(DB, T, H, MOBA_TOPK, MOBA_BLOCK).astype(vs.dtype)
    return (jnp.einsum('bthkl,bthkld->bthd', p_sel, vs)
            + jnp.einsum('bthl,btlhd->bthd', p_own.astype(vo.dtype), vo)
            + jnp.einsum('bths,bshd->bthd', p_new.astype(v_new.dtype), v_new))


def gla_chunked(q, k, v, log_a, s0):
    B, T, H, dk = q.shape
    dv = v.shape[-1]
    C = GLA_CHUNK if T % GLA_CHUNK == 0 else T
    n = T // C

    def to_chunks(a):
        return a.astype(jnp.float32).reshape(B, n, C, H, a.shape[-1]).swapaxes(0, 1)

    causal = jnp.tril(jnp.ones((C, C), dtype=bool))[None, :, :, None, None]

    def step(S, inp):
        qc, kc, vc, ac = inp
        A = jnp.cumsum(ac, axis=1)
        decay = jnp.exp(jnp.where(causal, A[:, :, None] - A[:, None, :], NEG))
        att = jnp.einsum('bthd,bshd,btshd->bths', qc, kc, decay)
        o = jnp.einsum('bths,bshv->bthv', att, vc) + jnp.einsum('bthk,bhkv->bthv', qc * jnp.exp(A), S)
        A_last = A[:, -1]
        S = jnp.exp(A_last)[..., None] * S + jnp.einsum('bshk,bshv->bhkv', kc * jnp.exp(A_last[:, None] - A), vc)
        return S, o

    S, o = lax.scan(step, s0.astype(jnp.float32), (to_chunks(q), to_chunks(k), to_chunks(v), to_chunks(log_a)))
    return o.swapaxes(0, 1).reshape(B, T, H, dv), S


def mixer_output(o_moba, o_gla, rg, g_gla_out, w_out):
    B, T = o_moba.shape[:2]
    og = rmsnorm(o_gla, g_gla_out).reshape(B, T, GLA_WIDTH) * jax.nn.silu(rg)
    return jnp.concatenate([o_moba.reshape(B, T, MOBA_WIDTH), og], axis=-1) @ w_out


def memory_kv(mem, g_mem_kv, w_mem_k, w_mem_v):
    B = mem.shape[0]
    m = rmsnorm(mem, g_mem_kv)
    mk = (m @ w_mem_k).reshape(B, MEM_LEN, MEM_HEADS, MEM_HEAD_DIM)
    mv = (m @ w_mem_v).reshape(B, MEM_LEN, MEM_HEADS, MEM_HEAD_DIM)
    return mk, mv


def memory_attend(h, mk, mv, w_mem_q, w_mem_o):
    B, T, _ = h.shape
    q = (h @ w_mem_q).reshape(B, T, MEM_HEADS, MEM_HEAD_DIM)
    s = jnp.einsum('bthd,bmhd->bhtm', q, mk).astype(jnp.float32) * (MEM_HEAD_DIM ** -0.5)
    p = jax.nn.softmax(s, axis=-1).astype(mv.dtype)
    o = jnp.einsum('bhtm,bmhd->bthd', p, mv).reshape(B, T, D_MODEL)
    return o @ w_mem_o


def peer_ffn(h, w_peer_q, peer_sub_keys, expert_u, expert_v):
    B, T, D = h.shape
    n = B * T
    chunk = PEER_TOKEN_CHUNK if n % PEER_TOKEN_CHUNK == 0 else n

    def route_and_mix(xc):
        c = xc.shape[0]
        qv = (xc @ w_peer_q).reshape(c, PEER_HEADS, 2, PEER_DQ // 2)
        s = jnp.einsum('chpd,phkd->chpk', qv, peer_sub_keys).astype(jnp.float32)
        top_s, top_i = lax.top_k(s, PEER_TOPK)
        cand_s = (top_s[:, :, 0, :, None] + top_s[:, :, 1, None, :]).reshape(c, PEER_HEADS, PEER_TOPK * PEER_TOPK)
        cand_i = (top_i[:, :, 0, :, None] * PEER_NKEYS + top_i[:, :, 1, None, :]).reshape(c, PEER_HEADS, PEER_TOPK * PEER_TOPK)
        fin_s, fin_pos = lax.top_k(cand_s, PEER_TOPK)
        eid = jnp.take_along_axis(cand_i, fin_pos, axis=-1)
        g = jax.nn.softmax(fin_s, axis=-1)
        u = expert_u[eid]
        a = jax.nn.gelu(jnp.einsum('chkd,cd->chk', u, xc).astype(jnp.float32), approximate=False)
        return jnp.einsum('chk,chkd->cd', (g * a).astype(xc.dtype), expert_v[eid])

    out = lax.map(route_and_mix, h.reshape(n // chunk, chunk, D))
    return out.reshape(B, T, D)


def decoder_layer(x, pos, moba_fn, gla_s0, mem_k, mem_v, g_mix, w_in, w_alpha_up, b_alpha, g_gla_out, w_out,
                  g_mem_q, w_mem_q, w_mem_o, g_ffn, w_peer_q, peer_sub_keys, expert_u, expert_v):
    h = rmsnorm(x, g_mix)
    qm, km, vm, qg, kg, vg, rg, log_a = mixer_inputs(h, pos, w_in, w_alpha_up, b_alpha)
    o_moba = moba_fn(qm, km, vm)
    o_gla, s_new = gla_chunked(qg, kg, vg, log_a, gla_s0)
    x = x + mixer_output(o_moba, o_gla.astype(x.dtype), rg, g_gla_out, w_out)
    x = x + memory_attend(rmsnorm(x, g_mem_q), mem_k, mem_v, w_mem_q, w_mem_o)
    x = x + peer_ffn(rmsnorm(x, g_ffn), w_peer_q, peer_sub_keys, expert_u, expert_v)
    return x, km, vm, s_new


def setup_inputs(seed: int = 0) -> dict:
    key = jax.random.key(seed)
    k = jax.random.split(key, 27)
    n_pages = PAST_LEN // PAGE_SIZE
    n_used = DEC_BATCH * n_pages
    n_pool = n_used + max(1, n_used // 4)
    L = DEPTH
    f32 = jnp.float32

    def nrm(kk, shape, scale=1.0):
        return scale * jax.random.normal(kk, shape, f32)

    def gain(kk, shape):
        return 1.0 + 0.01 * jax.random.normal(kk, shape, f32)

    return {
        'x_prompt': nrm(k[0], (BATCH, SEQ, D_MODEL)),
        'x_sample': nrm(k[1], (DEC_BATCH, DEC_SEQ, D_MODEL)),
        'cache_k': nrm(k[2], (L, n_pool, PAGE_SIZE, MOBA_HEADS, MOBA_HEAD_DIM)),
        'cache_v': nrm(k[3], (L, n_pool, PAGE_SIZE, MOBA_HEADS, MOBA_HEAD_DIM)),
        'state_gla': nrm(k[4], (L, DEC_BATCH, GLA_HEADS, GLA_DK, GLA_DV)),
        'cache_mem_k': nrm(k[5], (L, DEC_BATCH, MEM_LEN, MEM_HEADS, MEM_HEAD_DIM)),
        'cache_mem_v': nrm(k[6], (L, DEC_BATCH, MEM_LEN, MEM_HEADS, MEM_HEAD_DIM)),
        'page_table': jax.random.permutation(k[7], n_pool)[:n_used].reshape(DEC_BATCH, n_pages).astype(jnp.int32),
        'mem_prompt': nrm(k[8], (BATCH, MEM_LEN, D_MODEL)),
        'g_mix': gain(k[9], (L, D_MODEL)),
        'w_in': nrm(k[10], (L, D_MODEL, IN_COLS), D_MODEL ** -0.5),
        'w_alpha_up': nrm(k[11], (L, GLA_LOWRANK, GLA_KEY_WIDTH), GLA_LOWRANK ** -0.5),
        'b_alpha': nrm(k[12], (L, GLA_KEY_WIDTH), 0.1),
        'g_gla_out': gain(k[13], (L, GLA_DV)),
        'w_out': nrm(k[14], (L, MIX_WIDTH, D_MODEL), MIX_WIDTH ** -0.5),
        'g_mem_q': gain(k[15], (L, D_MODEL)),
        'g_mem_kv': gain(k[16], (L, D_MODEL)),
        'w_mem_q': nrm(k[17], (L, D_MODEL, D_MODEL), D_MODEL ** -0.5),
        'w_mem_k': nrm(k[18], (L, D_MODEL, D_MODEL), D_MODEL ** -0.5),
        'w_mem_v': nrm(k[19], (L, D_MODEL, D_MODEL), D_MODEL ** -0.5),
        'w_mem_o': nrm(k[20], (L, D_MODEL, D_MODEL), D_MODEL ** -0.5),
        'g_ffn': gain(k[21], (L, D_MODEL)),
        'w_peer_q': nrm(k[22], (L, D_MODEL, PEER_HEADS * PEER_DQ), D_MODEL ** -0.5),
        'peer_sub_keys': nrm(k[23], (L, 2, PEER_HEADS, PEER_NKEYS, PEER_DQ // 2), (PEER_DQ // 2) ** -0.5),
        'expert_u': nrm(k[24], (L, PEER_EXPERTS, D_MODEL), D_MODEL ** -0.5),
        'expert_v': nrm(k[25], (L, PEER_EXPERTS, D_MODEL), PEER_HEADS ** -0.5),
        'g_final': gain(k[26], (D_MODEL,)),
    }


def reference(x_prompt, x_sample, cache_k, cache_v, state_gla, cache_mem_k, cache_mem_v, page_table, mem_prompt,
              g_mix, w_in, w_alpha_up, b_alpha, g_gla_out, w_out, g_mem_q, g_mem_kv, w_mem_q, w_mem_k, w_mem_v,
              w_mem_o, g_ffn, w_peer_q, peer_sub_keys, expert_u, expert_v, g_final):
    B, S, _ = x_prompt.shape
    DB, T, _ = x_sample.shape
    past = page_table.shape[1] * PAGE_SIZE
    pos_prompt = jnp.arange(S)
    pos_sample = past + jnp.arange(T)
    xp, xs = x_prompt, x_sample
    kp_l, vp_l, sp_l, mkp_l, mvp_l, ks_l, vs_l, ss_l = [], [], [], [], [], [], [], []
    for l in range(DEPTH):
        shared = (g_mix[l], w_in[l], w_alpha_up[l], b_alpha[l], g_gla_out[l], w_out[l],
                  g_mem_q[l], w_mem_q[l], w_mem_o[l], g_ffn[l], w_peer_q[l], peer_sub_keys[l],
                  expert_u[l], expert_v[l])
        mk_p, mv_p = memory_kv(mem_prompt, g_mem_kv[l], w_mem_k[l], w_mem_v[l])
        s0 = jnp.zeros((B, GLA_HEADS, GLA_DK, GLA_DV), jnp.float32)
        xp, k_p, v_p, s_p = decoder_layer(xp, pos_prompt, moba_prompt, s0, mk_p, mv_p, *shared)
        ck, cv = cache_k[l], cache_v[l]
        xs, k_s, v_s, s_s = decoder_layer(
            xs, pos_sample, lambda q, kk, vv: moba_sample(q, kk, vv, ck, cv, page_table),
            state_gla[l], cache_mem_k[l], cache_mem_v[l], *shared)
        kp_l.append(k_p)
        vp_l.append(v_p)
        sp_l.append(s_p.astype(state_gla.dtype))
        mkp_l.append(mk_p)
        mvp_l.append(mv_p)
        ks_l.append(k_s)
        vs_l.append(v_s)
        ss_l.append(s_s.astype(state_gla.dtype))
    y_prompt = rmsnorm(xp, g_final)
    y_sample = rmsnorm(xs, g_final)
    k_prompt = jnp.stack(kp_l)
    v_prompt = jnp.stack(vp_l)
    gla_state_prompt = jnp.stack(sp_l)
    mem_k_prompt = jnp.stack(mkp_l)
    mem_v_prompt = jnp.stack(mvp_l)
    k_sample = jnp.stack(ks_l)
    v_sample = jnp.stack(vs_l)
    gla_state_sample = jnp.stack(ss_l)
    return (y_prompt, y_sample, k_prompt, v_prompt, gla_state_prompt, mem_k_prompt, mem_v_prompt,
            k_sample, v_sample, gla_state_sample)
```

```python
import functools
import math

import numpy as np
import jax
import jax.numpy as jnp
from jax import lax
from jax.experimental import pallas as pl
from jax.experimental.pallas import tpu as pltpu

F32 = jnp.float32
BF16 = jnp.bfloat16
I32 = jnp.int32

EPS = 1e-6
NEG = -1e30
LANES = 128
SUBLANES = 8
VMEM_LIMIT = 56 * 1024 * 1024

D_MODEL = 1024
MOBA_HEADS, MOBA_DH, MOBA_BLOCK, MOBA_TOPK = 8, 64, 256, 3
MOBA_W = MOBA_HEADS * MOBA_DH
ROPE_THETA, ROPE_DIM = 500000.0, 16
GLA_HEADS, GLA_DK, GLA_DV = 4, 64, 128
GLA_KW, GLA_W = GLA_HEADS * GLA_DK, GLA_HEADS * GLA_DV
GLA_LOWRANK, GLA_TAU, GLA_CHUNK = 16, 16.0, 64
IN_COLS = 3 * MOBA_W + 2 * GLA_KW + 2 * GLA_W + GLA_LOWRANK
IN_COLS_PAD = 3 * MOBA_W + 2 * GLA_KW + 2 * GLA_W + LANES
MEM_HEADS, MEM_DH = 4, 256
PEER_HEADS, PEER_NKEYS, PEER_DQ, PEER_TOPK = 8, 128, 256, 16
PEER_PAIRS = PEER_HEADS * PEER_TOPK
SAMPLE_TPAD = 8


def _params(sem):
    return pltpu.CompilerParams(dimension_semantics=sem, vmem_limit_bytes=VMEM_LIMIT)


def _dot(a, b):
    return jnp.dot(a, b, preferred_element_type=F32)


def _dot_nt(a, b):
    return lax.dot_general(a, b, (((1,), (1,)), ((), ())), preferred_element_type=F32)


def _dot_tn(a, b):
    return lax.dot_general(a, b, (((0,), (0,)), ((), ())), preferred_element_type=F32)


def _split3(x):
    x1 = x.astype(BF16)
    r1 = x - x1.astype(F32)
    x2 = r1.astype(BF16)
    x3 = (r1 - x2.astype(F32)).astype(BF16)
    return x1, x2, x3


def _dot_exact_lhs(sel, x):
    a, b, c = _split3(x)
    return _dot(sel, a) + _dot(sel, b) + _dot(sel, c)


def _rms(x, g):
    return x * lax.rsqrt(jnp.mean(x * x, axis=-1, keepdims=True) + EPS) * g


def _row_tile(n):
    return 256 if n % 256 == 0 else n


def _proj_kernel(x_ref, g_ref, w_ref, wa_ref, ba_ref, cos_ref, sa_ref, sb_ref,
                 q_ref, k_ref, v_ref, qg_ref, kg_ref, vg_ref, rg_ref, la_ref):
    h = _rms(x_ref[...], g_ref[...]).astype(BF16)
    cos, sa, sb = cos_ref[...], sa_ref[...], sb_ref[...]

    def rope(z):
        outs = []
        for c in range(MOBA_W // LANES):
            zc = z[:, c * LANES:(c + 1) * LANES]
            outs.append(zc * cos + pltpu.roll(zc, ROPE_DIM // 2, 1) * sa
                        + pltpu.roll(zc, LANES - ROPE_DIM // 2, 1) * sb)
        return jnp.concatenate(outs, axis=1)

    o = 0
    q = rope(_dot(h, w_ref[:, o:o + MOBA_W])); o += MOBA_W
    q_ref[...] = (q * (MOBA_DH ** -0.5)).astype(q_ref.dtype)
    k_ref[...] = rope(_dot(h, w_ref[:, o:o + MOBA_W])); o += MOBA_W
    v_ref[...] = _dot(h, w_ref[:, o:o + MOBA_W]); o += MOBA_W
    qg_ref[...] = _dot(h, w_ref[:, o:o + GLA_KW]) * (GLA_DK ** -0.5); o += GLA_KW
    kg_ref[...] = _dot(h, w_ref[:, o:o + GLA_KW]); o += GLA_KW
    vg_ref[...] = _dot(h, w_ref[:, o:o + GLA_W]); o += GLA_W
    rg_ref[...] = _dot(h, w_ref[:, o:o + GLA_W]); o += GLA_W
    ag = _dot(h, w_ref[:, o:o + LANES])
    xg = _dot(ag.astype(BF16), wa_ref[...]) + ba_ref[...]
    log_sig = jnp.minimum(xg, 0.0) - jnp.log(1.0 + jnp.exp(-jnp.abs(xg)))
    la_ref[...] = log_sig * (1.0 / GLA_TAU)


def _rope_tables(pos):
    half = ROPE_DIM // 2
    inv_freq = 1.0 / (ROPE_THETA ** (jnp.arange(half, dtype=F32) * (2.0 / ROPE_DIM)))
    ang = pos.astype(F32)[:, None] * inv_freq[None, :]
    cos, sin = jnp.cos(ang), jnp.sin(ang)
    n = pos.shape[0]
    one = jnp.ones((n, MOBA_DH - ROPE_DIM), F32)
    zero = jnp.zeros((n, MOBA_DH - ROPE_DIM), F32)
    zh = jnp.zeros((n, half), F32)
    c64 = jnp.concatenate([cos, cos, one], axis=1)
    sa64 = jnp.concatenate([zh, sin, zero], axis=1)
    sb64 = jnp.concatenate([-sin, zh, zero], axis=1)
    rep = LANES // MOBA_DH
    return (jnp.tile(c64, (1, rep)), jnp.tile(sa64, (1, rep)), jnp.tile(sb64, (1, rep)))


def _mixer_inputs(x2d, pos_rows, g_mix, w_in_p, wa_p, b_alpha, q_dtype):
    n = x2d.shape[0]
    tm = _row_tile(n)
    nrep = pos_rows.shape[0] // tm
    cos, sa, sb = _rope_tables(pos_rows)
    row = lambda w: pl.BlockSpec((tm, w), lambda i: (i, 0))
    full = lambda a: pl.BlockSpec(a.shape, lambda i: (0, 0))
    tab = pl.BlockSpec((tm, LANES), lambda i: (i % nrep, 0))
    g2 = g_mix.reshape(1, D_MODEL)
    b2 = b_alpha.reshape(1, GLA_KW)
    widths = (MOBA_W, MOBA_W, MOBA_W, GLA_KW, GLA_KW, GLA_W, GLA_W, GLA_KW)
    dts = (q_dtype, F32, F32, F32, F32, F32, F32, F32)
    return pl.pallas_call(
        _proj_kernel,
        grid=(n // tm,),
        in_specs=[row(D_MODEL), full(g2), full(w_in_p), full(wa_p), full(b2), tab, tab, tab],
        out_specs=[row(w) for w in widths],
        out_shape=[jax.ShapeDtypeStruct((n, w), dt) for w, dt in zip(widths, dts)],
        compiler_params=_params(("arbitrary",)),
        name="mixer_inputs",
    )(x2d, g2, w_in_p, wa_p, b2, cos, sa, sb)


def _moba_prompt_kernel(q_ref, k_ref, v_ref, o_ref, km_ref, m_ref, l_ref, acc_ref, *, nblk):
    i = pl.program_id(2)
    blk = MOBA_BLOCK
    hpl = LANES // MOBA_DH

    @pl.when(i == 0)
    def _():
        for j in range(nblk):
            km_ref[j:j + 1, :] = jnp.mean(k_ref[0, j * blk:(j + 1) * blk, :], axis=0, keepdims=True)

    q = q_ref[0]
    jrow = lax.broadcasted_iota(I32, (nblk, blk), 0)
    rowi = lax.broadcasted_iota(I32, (blk, blk), 0)
    coli = lax.broadcasted_iota(I32, (blk, blk), 1)
    off = pl.multiple_of(i * blk, blk)
    for hh in range(hpl):
        ls = slice(hh * MOBA_DH, (hh + 1) * MOBA_DH)
        qh = q[:, ls]
        km = km_ref[:, ls]
        k1, k2, k3 = _split3(km)
        gt = _dot_nt(k1, qh) + _dot_nt(k2, qh) + _dot_nt(k3, qh)
        gm = jnp.where(jrow < i, gt, NEG)
        cnt = jnp.zeros((nblk, blk), I32)
        for jp in range(nblk):
            gj = gm[jp:jp + 1, :]
            beats = (gj > gm) | ((gj == gm) & (jp < jrow))
            cnt = cnt + beats.astype(I32)
        sel_t = ((cnt < MOBA_TOPK) & (jrow < i)).astype(F32)
        eye = (lax.broadcasted_iota(I32, (nblk, nblk), 0) == lax.broadcasted_iota(I32, (nblk, nblk), 1))
        sel = _dot_tn(sel_t.astype(BF16), eye.astype(BF16))

        kd = k_ref[0, pl.ds(off, blk), ls].astype(BF16)
        vd = v_ref[0, pl.ds(off, blk), ls].astype(BF16)
        s = _dot_nt(qh, kd)
        s = jnp.where(coli <= rowi, s, NEG)
        m0 = jnp.max(s, axis=1, keepdims=True)
        p = jnp.exp(s - m0)
        m_ref[hh] = m0
        l_ref[hh] = jnp.sum(p, axis=1, keepdims=True)
        acc_ref[hh] = _dot(p.astype(BF16), vd)

        for j in range(nblk - 1):
            @pl.when(j < i)
            def _(j=j, hh=hh, ls=ls, qh=qh, sel=sel):
                kj = k_ref[0, j * blk:(j + 1) * blk, ls].astype(BF16)
                vj = v_ref[0, j * blk:(j + 1) * blk, ls].astype(BF16)
                sj = _dot_nt(qh, kj)
                sj = jnp.where(sel[:, j:j + 1] > 0.5, sj, NEG)
                m_old = m_ref[hh]
                m_new = jnp.maximum(m_old, jnp.max(sj, axis=1, keepdims=True))
                alpha = jnp.exp(m_old - m_new)
                pj = jnp.exp(sj - m_new)
                l_ref[hh] = alpha * l_ref[hh] + jnp.sum(pj, axis=1, keepdims=True)
                acc_ref[hh] = alpha * acc_ref[hh] + _dot(pj.astype(BF16), vj)
                m_ref[hh] = m_new

    o_ref[0] = jnp.concatenate([acc_ref[hh] / l_ref[hh] for hh in range(hpl)], axis=1)


def _moba_prompt(q, k, v):
    b, s, w = q.shape
    assert s % MOBA_BLOCK == 0 and w == MOBA_W
    nblk = s // MOBA_BLOCK
    assert nblk % SUBLANES == 0 or nblk < SUBLANES
    hpl = LANES // MOBA_DH
    kv_spec = pl.BlockSpec((1, s, LANES), lambda bi, hp, i: (bi, 0, hp))
    q_spec = pl.BlockSpec((1, MOBA_BLOCK, LANES), lambda bi, hp, i: (bi, i, hp))
    return pl.pallas_call(
        functools.partial(_moba_prompt_kernel, nblk=nblk),
        grid=(b, w // LANES, nblk),
        in_specs=[q_spec, kv_spec, kv_spec],
        out_specs=q_spec,
        out_shape=jax.ShapeDtypeStruct((b, s, w), F32),
        scratch_shapes=[pltpu.VMEM((nblk, LANES), F32),
                        pltpu.VMEM((hpl, MOBA_BLOCK, 1), F32),
                        pltpu.VMEM((hpl, MOBA_BLOCK, 1), F32),
                        pltpu.VMEM((hpl, MOBA_BLOCK, MOBA_DH), F32)],
        compiler_params=_params(("arbitrary", "arbitrary", "arbitrary")),
        name="moba_prompt",
    )(q, k, v)


def _gla_consts(c):
    nlev = int(math.log2(c))
    assert 2 ** nlev == c
    r = np.arange(c)
    tri = (r[:, None] >= r[None, :]).astype(np.float32)
    sel = np.zeros((nlev, c, c), np.float32)
    msk = np.zeros((nlev + 1, c, c), np.float32)
    for l in range(nlev):
        b = 2 ** l
        ref_row = (r // (2 * b)) * (2 * b) + b - 1
        sel[l, r, ref_row] = 1.0
        same = (r[:, None] // (2 * b)) == (r[None, :] // (2 * b))
        msk[l] = same & ((r[:, None] % (2 * b)) >= b) & ((r[None, :] % (2 * b)) < b)
    msk[nlev] = np.eye(c)
    return jnp.asarray(tri, BF16), jnp.asarray(sel, BF16), jnp.asarray(msk, F32)


def _gla_kernel(q_ref, k_ref, v_ref, la_ref, s0_ref, tri_ref, sel_ref, msk_ref, o_ref, so_ref, st_ref,
                *, c, nchunk, nlev, t_real, nsteps):
    step = pl.program_id(1)

    @pl.when(step == 0)
    def _():
        st_ref[...] = s0_ref[0]

    eye = (lax.broadcasted_iota(I32, (GLA_DK, GLA_DK), 0)
           == lax.broadcasted_iota(I32, (GLA_DK, GLA_DK), 1)).astype(F32)
    for ci in range(nchunk):
        rows = slice(ci * c, (ci + 1) * c)
        q, k, v, la = q_ref[rows, :], k_ref[rows, :], v_ref[rows, :], la_ref[rows, :]
        if t_real < c:
            live = lax.broadcasted_iota(I32, (c, 1), 0) < t_real
            la = jnp.where(live, la, 0.0)
            k = jnp.where(live, k, 0.0)
        a = _dot_exact_lhs(tri_ref[...], la)
        a_last = a[c - 1:c, :]
        qe = (q * jnp.exp(a)).astype(BF16)
        kd = (k * jnp.exp(a_last - a)).astype(BF16)
        e_last = jnp.exp(a_last)
        qb, kb, vb = q.astype(BF16), k.astype(BF16), v.astype(BF16)
        qt, kt = [], []
        for l in range(nlev):
            aref = _dot_exact_lhs(sel_ref[l], a)
            qt.append((q * jnp.exp(jnp.minimum(a - aref, 0.0))).astype(BF16))
            kt.append((k * jnp.exp(jnp.minimum(aref - a, 0.0))).astype(BF16))
        outs = []
        for h in range(GLA_HEADS):
            hs = slice(h * GLA_DK, (h + 1) * GLA_DK)
            vs = slice(h * GLA_DV, (h + 1) * GLA_DV)
            att = msk_ref[nlev] * _dot_nt(qb[:, hs], kb[:, hs])
            for l in range(nlev):
                att = att + msk_ref[l] * _dot_nt(qt[l][:, hs], kt[l][:, hs])
            sh = st_ref[h]
            outs.append(_dot(att.astype(BF16), vb[:, vs]) + _dot(qe[:, hs], sh.astype(BF16)))
            e_col = jnp.sum(eye * e_last[:, hs], axis=1, keepdims=True)
            st_ref[h] = e_col * sh + _dot_tn(kd[:, hs], vb[:, vs])
        o_ref[rows, :] = jnp.concatenate(outs, axis=1)

    @pl.when(step == nsteps - 1)
    def _():
        so_ref[0] = st_ref[...]


def _gla(qg, kg, vg, la, s0, b, t, c, rows_per_step, t_real):
    assert t % rows_per_step == 0 and rows_per_step % c == 0
    nsteps = t // rows_per_step
    nlev = int(math.log2(c))
    tri, sel, msk = _gla_consts(c)
    row = lambda w: pl.BlockSpec((rows_per_step, w), lambda bi, i: (bi * nsteps + i, 0))
    st = pl.BlockSpec((1, GLA_HEADS, GLA_DK, GLA_DV), lambda bi, i: (bi, 0, 0, 0))
    const2 = pl.BlockSpec(tri.shape, lambda bi, i: (0, 0))
    const3 = lambda a: pl.BlockSpec(a.shape, lambda bi, i: (0, 0, 0))
    return pl.pallas_call(
        functools.partial(_gla_kernel, c=c, nchunk=rows_per_step // c, nlev=nlev, t_real=t_real,
                          nsteps=nsteps),
        grid=(b, nsteps),
        in_specs=[row(GLA_KW), row(GLA_KW), row(GLA_W), row(GLA_KW), st, const2, const3(sel), const3(msk)],
        out_specs=[row(GLA_W), st],
        out_shape=[jax.ShapeDtypeStruct((b * t, GLA_W), F32),
                   jax.ShapeDtypeStruct((b, GLA_HEADS, GLA_DK, GLA_DV), F32)],
        scratch_shapes=[pltpu.VMEM((GLA_HEADS, GLA_DK, GLA_DV), F32)],
        compiler_params=_params(("arbitrary", "arbitrary")),
        name="gla",
    )(qg, kg, vg, la, s0, tri, sel, msk)


def _mixer_out_kernel(x_ref, om_ref, og_ref, rg_ref, gg_ref, wo_ref, gq_ref, wq_ref, x1_ref, qm_ref):
    og = og_ref[...]
    parts = []
    for h in range(GLA_HEADS):
        oh = og[:, h * GLA_DV:(h + 1) * GLA_DV]
        parts.append(_rms(oh, gg_ref[...]))
    rg = rg_ref[...]
    ogn = jnp.concatenate(parts, axis=1) * (rg * jax.nn.sigmoid(rg))
    cat = jnp.concatenate([om_ref[...], ogn], axis=1).astype(BF16)
    x1 = x_ref[...] + _dot(cat, wo_ref[...])
    x1_ref[...] = x1
    qm_ref[...] = _dot(_rms(x1, gq_ref[...]).astype(BF16), wq_ref[...]).astype(qm_ref.dtype)


def _mixer_out(x2d, o_moba, o_gla, rg, g_gla_out, w_out, g_mem_q, w_mem_q):
    n = x2d.shape[0]
    tm = _row_tile(n)
    row = lambda w: pl.BlockSpec((tm, w), lambda i: (i, 0))
    full = lambda a: pl.BlockSpec(a.shape, lambda i: (0, 0))
    gg = g_gla_out.reshape(1, GLA_DV)
    gq = g_mem_q.reshape(1, D_MODEL)
    return pl.pallas_call(
        _mixer_out_kernel,
        grid=(n // tm,),
        in_specs=[row(D_MODEL), row(MOBA_W), row(GLA_W), row(GLA_W), full(gg), full(w_out), full(gq),
                  full(w_mem_q)],
        out_specs=[row(D_MODEL), row(D_MODEL)],
        out_shape=[jax.ShapeDtypeStruct((n, D_MODEL), F32), jax.ShapeDtypeStruct((n, D_MODEL), BF16)],
        compiler_params=_params(("arbitrary",)),
        name="mixer_out",
    )(x2d, o_moba, o_gla, rg, gg, w_out, gq, w_mem_q)


def _mem_kv_kernel(m_ref, g_ref, wk_ref, wv_ref, mk_ref, mv_ref):
    h = _rms(m_ref[...], g_ref[...]).astype(BF16)
    mk_ref[...] = _dot(h, wk_ref[...])
    mv_ref[...] = _dot(h, wv_ref[...])


def _mem_kv(mem2d, g_mem_kv, w_mem_k, w_mem_v):
    n = mem2d.shape[0]
    tm = _row_tile(n)
    row = pl.BlockSpec((tm, D_MODEL), lambda i: (i, 0))
    full = lambda a: pl.BlockSpec(a.shape, lambda i: (0, 0))
    g = g_mem_kv.reshape(1, D_MODEL)
    return pl.pallas_call(
        _mem_kv_kernel,
        grid=(n // tm,),
        in_specs=[row, full(g), full(w_mem_k), full(w_mem_v)],
        out_specs=[row, row],
        out_shape=[jax.ShapeDtypeStruct((n, D_MODEL), F32)] * 2,
        compiler_params=_params(("arbitrary",)),
        name="memory_kv",
    )(mem2d, g, w_mem_k, w_mem_v)


def _mem_attend_kernel(q_ref, mk_ref, mv_ref, x1_ref, wo_ref, gf_ref, x2_ref, h3_ref):
    q = q_ref[...]
    mk = mk_ref[0].astype(BF16)
    mv = mv_ref[0].astype(BF16)
    outs = []
    for h in range(MEM_HEADS):
        hs = slice(h * MEM_DH, (h + 1) * MEM_DH)
        s = _dot_nt(q[:, hs], mk[:, hs]) * (MEM_DH ** -0.5)
        p = jnp.exp(s - jnp.max(s, axis=1, keepdims=True))
        l = jnp.sum(p, axis=1, keepdims=True)
        outs.append(_dot(p.astype(BF16), mv[:, hs]) / l)
    o = jnp.concatenate(outs, axis=1).astype(BF16)
    x2 = x1_ref[...] + _dot(o, wo_ref[...])
    x2_ref[...] = x2
    h3_ref[...] = _rms(x2, gf_ref[...])


def _mem_attend(qmem, mk, mv, x1, w_mem_o, g_ffn, b, t):
    tq = _row_tile(t)
    nq = t // tq
    row = pl.BlockSpec((tq, D_MODEL), lambda bi, i: (bi * nq + i, 0))
    mem = pl.BlockSpec((1,) + mk.shape[1:], lambda bi, i: (bi, 0, 0))
    full = lambda a: pl.BlockSpec(a.shape, lambda bi, i: (0, 0))
    gf = g_ffn.reshape(1, D_MODEL)
    return pl.pallas_call(
        _mem_attend_kernel,
        grid=(b, nq),
        in_specs=[row, mem, mem, row, full(w_mem_o), full(gf)],
        out_specs=[row, row],
        out_shape=[jax.ShapeDtypeStruct((b * t, D_MODEL), F32)] * 2,
        compiler_params=_params(("arbitrary", "arbitrary")),
        name="memory_attend",
    )(qmem, mk, mv, x1, w_mem_o, gf)


def _peer_candidates():
    k = PEER_TOPK
    flat = []
    flat += [0 * k + j for j in range(k)]
    for i in range(1, 8):
        flat += [i * k + j if (i + 1) * (j + 1) <= k else -1 for j in range(8)]
    flat += [(8 + r) * k for r in range(8)]
    return np.asarray(flat, np.int32)


def _route_kernel(h_ref, wq_ref, keys_ref, flat_ref, eid_ref, g_ref):
    tq = h_ref.shape[0]
    k = PEER_TOPK
    hb = h_ref[...].astype(BF16)
    kio = lax.broadcasted_iota(I32, (PEER_NKEYS, tq), 0)
    r16 = lax.broadcasted_iota(I32, (k, tq), 0)
    flat = flat_ref[...]
    ncand = flat.shape[0]
    big = jnp.int32(1 << 20)

    def top16(s):
        def body(kk, carry):
            s, vals, idxs = carry
            m = jnp.max(s, axis=0, keepdims=True)
            idx = jnp.min(jnp.where(s == m, kio, PEER_NKEYS), axis=0, keepdims=True)
            vals = jnp.where(r16 == kk, m, vals)
            idxs = jnp.where(r16 == kk, idx, idxs)
            s = jnp.where(kio == idx, -jnp.inf, s)
            return s, vals, idxs
        _, vals, idxs = lax.fori_loop(0, k, body, (s, jnp.zeros((k, tq), F32), jnp.zeros((k, tq), I32)))
        return vals, idxs

    for h in range(PEER_HEADS):
        tv, ti = [], []
        for p in range(2):
            hp = h * 2 + p
            qv = _dot(hb, wq_ref[hp]).astype(BF16)
            s = _dot_nt(keys_ref[hp], qv)
            vals, idxs = top16(s)
            tv.append(vals)
            ti.append(idxs)
        a, b = tv
        ia, ib = ti
        cs = [a[0:1] + b]
        ce = [ia[0:1] * PEER_NKEYS + ib]
        for i in range(1, 8):
            cs.append(a[i:i + 1] + b[0:8])
            ce.append(ia[i:i + 1] * PEER_NKEYS + ib[0:8])
        cs.append(a[8:16] + b[0:1])
        ce.append(ia[8:16] * PEER_NKEYS + ib[0:1])
        cand = jnp.where(flat >= 0, jnp.concatenate(cs, axis=0), -jnp.inf)
        cexp = jnp.concatenate(ce, axis=0)

        def body2(kk, carry):
            cand, vals, eids = carry
            m = jnp.max(cand, axis=0, keepdims=True)
            pick = jnp.min(jnp.where((cand == m) & (flat >= 0), flat, big), axis=0, keepdims=True)
            hit = flat == pick
            e = jnp.max(jnp.where(hit, cexp, -1), axis=0, keepdims=True)
            vals = jnp.where(r16 == kk, m, vals)
            eids = jnp.where(r16 == kk, e, eids)
            cand = jnp.where(hit, -jnp.inf, cand)
            return cand, vals, eids
        _, fin, eids = lax.fori_loop(0, k, body2, (cand, jnp.zeros((k, tq), F32), jnp.zeros((k, tq), I32)))
        ex = jnp.exp(fin - fin[0:1])
        g_ref[h * k:(h + 1) * k, :] = ex / jnp.sum(ex, axis=0, keepdims=True)
        eid_ref[h * k:(h + 1) * k, :] = eids


def _route(h3, wq_hp, keys_hp):
    n = h3.shape[0]
    tq = _row_tile(n)
    flat = jnp.broadcast_to(jnp.asarray(_peer_candidates())[:, None], (_peer_candidates().shape[0], tq))
    out = pl.BlockSpec((PEER_PAIRS, tq), lambda i: (0, i))
    return pl.pallas_call(
        _route_kernel,
        grid=(n // tq,),
        in_specs=[pl.BlockSpec((tq, D_MODEL), lambda i: (i, 0)),
                  pl.BlockSpec(wq_hp.shape, lambda i: (0, 0, 0)),
                  pl.BlockSpec(keys_hp.shape, lambda i: (0, 0, 0)),
                  pl.BlockSpec(flat.shape, lambda i: (0, 0))],
        out_specs=[out, out],
        out_shape=[jax.ShapeDtypeStruct((PEER_PAIRS, n), I32), jax.ShapeDtypeStruct((PEER_PAIRS, n), F32)],
        compiler_params=_params(("arbitrary",)),
        name="peer_route",
    )(h3, wq_hp, keys_hp, flat)


PEER_TT = 8
ROW_WORDS = D_MODEL // 2
ROW_SUB = ROW_WORDS // LANES
REC_SUB = 2 * ROW_SUB
assert REC_SUB == SUBLANES


def _pack_experts(u, v):
    def words(t):
        tb = lax.bitcast_convert_type(t.astype(BF16), jnp.uint16).astype(jnp.uint32)
        w = tb[:, :ROW_WORDS] | (tb[:, ROW_WORDS:] << 16)
        return w.reshape(t.shape[0], ROW_SUB, LANES)
    tab = jnp.concatenate([words(u), words(v)], axis=1)
    return lax.bitcast_convert_type(tab, I32).reshape(u.shape[0] * REC_SUB, LANES)


def _unpack(w):
    lo = lax.bitcast_convert_type(w << 16, F32)
    hi = lax.bitcast_convert_type(w & jnp.int32(-65536), F32)
    return lo, hi


def _peer_kernel(eid_ref, eidn_ref, g_ref, h_ref, x2_ref, gf_ref, tab_ref, y_ref, buf_ref, sem, po_ref,
                 *, nsteps):
    i = pl.program_id(0)
    slot = i % 2
    nslot = 1 - slot
    tile_rows = PEER_TT * PEER_PAIRS * REC_SUB

    def gather(eref, sl, t):
        for p in range(PEER_PAIRS):
            src = pl.multiple_of(eref[t, p] * REC_SUB, REC_SUB)
            dst = pl.multiple_of((t * PEER_PAIRS + p) * REC_SUB, REC_SUB)
            pltpu.make_async_copy(tab_ref.at[pl.ds(src, REC_SUB), :],
                                  buf_ref.at[sl, pl.ds(dst, REC_SUB), :], sem.at[sl]).start()

    def wait_tile(sl):
        pltpu.make_async_copy(tab_ref.at[pl.ds(0, tile_rows), :], buf_ref.at[sl], sem.at[sl]).wait()

    @pl.when(i == 0)
    def _():
        def first(t, c):
            gather(eid_ref, 0, t)
            return c
        lax.fori_loop(0, PEER_TT, first, 0)

    wait_tile(slot)
    eye = (lax.broadcasted_iota(I32, (PEER_PAIRS, PEER_PAIRS), 0)
           == lax.broadcasted_iota(I32, (PEER_PAIRS, PEER_PAIRS), 1)).astype(F32)

    def token(t, c, prefetch):
        if prefetch:
            gather(eidn_ref, nslot, t)
        base = t * (PEER_PAIRS * REC_SUB)
        x = h_ref[pl.ds(t, 1), :]
        dacc = jnp.zeros((PEER_PAIRS, LANES), F32)
        for s in range(ROW_SUB):
            lo, hi = _unpack(buf_ref[slot, pl.ds(base + s, PEER_PAIRS, stride=REC_SUB), :])
            dacc = dacc + lo * x[:, s * LANES:(s + 1) * LANES] \
                + hi * x[:, ROW_WORDS + s * LANES:ROW_WORDS + (s + 1) * LANES]
        d = jnp.sum(dacc, axis=1, keepdims=True)
        act = 0.5 * d * (1.0 + lax.erf(d * (2.0 ** -0.5)))
        gcol = jnp.sum(eye * g_ref[pl.ds(t, 1), :], axis=1, keepdims=True)
        wcol = gcol * act
        los, his = [], []
        for s in range(ROW_SUB, REC_SUB):
            lo, hi = _unpack(buf_ref[slot, pl.ds(base + s, PEER_PAIRS, stride=REC_SUB), :])
            los.append(jnp.sum(wcol * lo, axis=0, keepdims=True))
            his.append(jnp.sum(wcol * hi, axis=0, keepdims=True))
        po_ref[pl.ds(t, 1), :] = jnp.concatenate(los + his, axis=1)
        return c

    n_prefetch = jnp.where(i + 1 < nsteps, PEER_TT, 0)
    lax.fori_loop(0, n_prefetch, functools.partial(token, prefetch=True), 0)
    lax.fori_loop(n_prefetch, PEER_TT, functools.partial(token, prefetch=False), 0)
    y_ref[...] = _rms(x2_ref[...] + po_ref[...], gf_ref[...])


def _peer_mix(eid, g, h3, x2, g_final, table):
    n = h3.shape[0]
    assert n % PEER_TT == 0
    nsteps = n // PEER_TT
    row = lambda w: pl.BlockSpec((PEER_TT, w), lambda i: (i, 0))
    gf = g_final.reshape(1, D_MODEL)
    return pl.pallas_call(
        functools.partial(_peer_kernel, nsteps=nsteps),
        grid=(nsteps,),
        in_specs=[pl.BlockSpec((PEER_TT, PEER_PAIRS), lambda i: (i, 0), memory_space=pltpu.SMEM),
                  pl.BlockSpec((PEER_TT, PEER_PAIRS), lambda i: (jnp.minimum(i + 1, nsteps - 1), 0),
                               memory_space=pltpu.SMEM),
                  row(PEER_PAIRS), row(D_MODEL), row(D_MODEL),
                  pl.BlockSpec((1, D_MODEL), lambda i: (0, 0)),
                  pl.BlockSpec(memory_space=pl.ANY)],
        out_specs=row(D_MODEL),
        out_shape=jax.ShapeDtypeStruct((n, D_MODEL), F32),
        scratch_shapes=[pltpu.VMEM((2, PEER_TT * PEER_PAIRS * REC_SUB, LANES), I32),
                        pltpu.SemaphoreType.DMA((2,)),
                        pltpu.VMEM((PEER_TT, D_MODEL), F32)],
        compiler_params=_params(("arbitrary",)),
        name="peer_mix",
    )(eid, eid, g, h3, x2, gf, table)


def _page_sum_kernel(k_ref, o_ref):
    for pg in range(k_ref.shape[0]):
        o_ref[pg:pg + 1, :] = jnp.sum(k_ref[pg], axis=0, keepdims=True)


def _page_sums(cache_k3):
    p, ps, w = cache_k3.shape
    pb = max(d for d in range(1, 33) if p % d == 0 and (d % SUBLANES == 0 or d == p))
    return pl.pallas_call(
        _page_sum_kernel,
        grid=(p // pb,),
        in_specs=[pl.BlockSpec((pb, ps, w), lambda i: (i, 0, 0))],
        out_specs=pl.BlockSpec((pb, w), lambda i: (i, 0)),
        out_shape=jax.ShapeDtypeStruct((p, w), F32),
        compiler_params=_params(("arbitrary",)),
        name="page_sums",
    )(cache_k3)


def _sample_gate_kernel(pt_ref, ps_ref, q_ref, sel_ref, km_ref, *, nfull, ppb):
    b = pl.program_id(0)
    for j in range(nfull):
        acc = ps_ref[pl.ds(pt_ref[b, j * ppb], 1), :]
        for pp in range(1, ppb):
            acc = acc + ps_ref[pl.ds(pt_ref[b, j * ppb + pp], 1), :]
        km_ref[j:j + 1, :] = acc * (1.0 / MOBA_BLOCK)
    q = q_ref[...].astype(BF16)
    lane = lax.broadcasted_iota(I32, (SAMPLE_TPAD, nfull), 1)
    olane = lax.broadcasted_iota(I32, (SAMPLE_TPAD, LANES), 1)
    out = jnp.zeros((SAMPLE_TPAD, LANES), I32)
    for h in range(MOBA_HEADS):
        ls = slice(h * MOBA_DH, (h + 1) * MOBA_DH)
        k1, k2, k3 = _split3(km_ref[:, ls])
        qh = q[:, ls]
        g = _dot_nt(qh, k1) + _dot_nt(qh, k2) + _dot_nt(qh, k3)
        for kk in range(MOBA_TOPK):
            m = jnp.max(g, axis=1, keepdims=True)
            idx = jnp.min(jnp.where(g == m, lane, nfull), axis=1, keepdims=True)
            out = out + jnp.where(olane == h * MOBA_TOPK + kk, idx, 0)
            g = jnp.where(lane == idx, -jnp.inf, g)
    sel_ref[...] = out


def _sample_gate(page_table, page_sum, q_s, nfull, ppb):
    db = page_table.shape[0]
    return pl.pallas_call(
        functools.partial(_sample_gate_kernel, nfull=nfull, ppb=ppb),
        grid_spec=pltpu.PrefetchScalarGridSpec(
            num_scalar_prefetch=1,
            grid=(db,),
            in_specs=[pl.BlockSpec(page_sum.shape, lambda b, pt: (0, 0)),
                      pl.BlockSpec((SAMPLE_TPAD, MOBA_W), lambda b, pt: (b, 0))],
            out_specs=pl.BlockSpec((SAMPLE_TPAD, LANES), lambda b, pt: (b, 0)),
            scratch_shapes=[pltpu.VMEM((nfull, MOBA_W), F32)]),
        out_shape=jax.ShapeDtypeStruct((db * SAMPLE_TPAD, LANES), I32),
        compiler_params=_params(("arbitrary",)),
        name="sample_gate",
    )(page_table, page_sum, q_s)


def _sample_attn_kernel(pt_ref, sel_ref, q_ref, kn_ref, vn_ref, ck_ref, cv_ref, o_ref, kbuf, vbuf, sem,
                        *, t_real, ppb, page):
    b = pl.program_id(0)
    hp = pl.program_id(1)
    hpl = LANES // MOBA_DH
    nslab = t_real * hpl * MOBA_TOPK * ppb
    lane0 = pl.multiple_of(hp * LANES, LANES)

    def copies(n):
        t, r = divmod(n, hpl * MOBA_TOPK * ppb)
        hh, r = divmod(r, MOBA_TOPK * ppb)
        kk, pp = divmod(r, ppb)
        blk = sel_ref[b, (t * MOBA_HEADS + hp * hpl + hh) * MOBA_TOPK + kk]
        phys = pt_ref[b, blk * ppb + pp]
        return (pltpu.make_async_copy(ck_ref.at[phys, :, pl.ds(lane0, LANES)], kbuf.at[n], sem.at[0]),
                pltpu.make_async_copy(cv_ref.at[phys, :, pl.ds(lane0, LANES)], vbuf.at[n], sem.at[1]))

    for n in range(nslab):
        ck, cv = copies(n)
        ck.start()
        cv.start()
    for n in range(nslab):
        ck, cv = copies(n)
        ck.wait()
        cv.wait()

    q = q_ref[...].astype(BF16)
    kn = kn_ref[...].astype(BF16)
    vn = vn_ref[...].astype(BF16)
    rows = lax.broadcasted_iota(I32, (SAMPLE_TPAD, SAMPLE_TPAD), 0)
    cols = lax.broadcasted_iota(I32, (SAMPLE_TPAD, SAMPLE_TPAD), 1)
    rowc = lax.broadcasted_iota(I32, (SAMPLE_TPAD, 1), 0)
    nsel = MOBA_TOPK * ppb
    outs = []
    for hh in range(hpl):
        ls = slice(hh * MOBA_DH, (hh + 1) * MOBA_DH)
        qh = q[:, ls]
        s_new = jnp.where((cols <= rows) & (cols < t_real), _dot_nt(qh, kn[:, ls]), NEG)
        m_new = jnp.max(s_new, axis=1, keepdims=True)
        acc = jnp.zeros((SAMPLE_TPAD, MOBA_DH), F32)
        for t in range(t_real):
            n0 = (t * hpl + hh) * nsel
            kc = jnp.concatenate([kbuf[n0 + r][:, ls] for r in range(nsel)], axis=0).astype(BF16)
            vc = jnp.concatenate([vbuf[n0 + r][:, ls] for r in range(nsel)], axis=0).astype(BF16)
            s = _dot_nt(qh, kc)
            m = jnp.maximum(jnp.max(s, axis=1, keepdims=True), m_new)
            p = jnp.exp(s - m)
            pn = jnp.exp(s_new - m)
            l = jnp.sum(p, axis=1, keepdims=True) + jnp.sum(pn, axis=1, keepdims=True)
            o = (_dot(p.astype(BF16), vc) + _dot(pn.astype(BF16), vn[:, ls])) / l
            acc = acc + jnp.where(rowc == t, o, 0.0)
        outs.append(acc)
    o_ref[...] = jnp.concatenate(outs, axis=1)


def _sample_attn(page_table, sel, q_s, k_new, v_new, cache_k3, cache_v3, t_real, ppb):
    db = page_table.shape[0]
    page = cache_k3.shape[1]
    hpl = LANES // MOBA_DH
    nslab = t_real * hpl * MOBA_TOPK * ppb
    row = pl.BlockSpec((SAMPLE_TPAD, LANES), lambda b, hp, pt, sl: (b, hp))
    anyspec = pl.BlockSpec(memory_space=pl.ANY)
    return pl.pallas_call(
        functools.partial(_sample_attn_kernel, t_real=t_real, ppb=ppb, page=page),
        grid_spec=pltpu.PrefetchScalarGridSpec(
            num_scalar_prefetch=2,
            grid=(db, MOBA_W // LANES),
            in_specs=[row, row, row, anyspec, anyspec],
            out_specs=row,
            scratch_shapes=[pltpu.VMEM((nslab, page, LANES), F32),
                            pltpu.VMEM((nslab, page, LANES), F32),
                            pltpu.SemaphoreType.DMA((2,))]),
        out_shape=jax.ShapeDtypeStruct((db * SAMPLE_TPAD, MOBA_W), F32),
        compiler_params=_params(("arbitrary", "arbitrary")),
        name="sample_attn",
    )(page_table, sel, q_s, k_new, v_new, cache_k3, cache_v3)


def _layer_tail(x2d, o_moba, o_gla, rg, b, t, mk, mv, wts, live_rows, g_last):
    x1, qmem = _mixer_out(x2d, o_moba, o_gla, rg, wts["g_gla_out"], wts["w_out"], wts["g_mem_q"], wts["w_mem_q"])
    x2, h3 = _mem_attend(qmem, mk, mv, x1, wts["w_mem_o"], wts["g_ffn"], b, t)
    if live_rows is not None:
        x2 = x2.reshape(b, t, D_MODEL)[:, :live_rows].reshape(b * live_rows, D_MODEL)
        h3 = h3.reshape(b, t, D_MODEL)[:, :live_rows].reshape(b * live_rows, D_MODEL)
    eid_t, g_t = _route(h3, wts["wq_hp"], wts["keys_hp"])
    return _peer_mix(eid_t.T, g_t.T, h3, x2, g_last, wts["table"])


def kernel(x_prompt, x_sample, cache_k, cache_v, state_gla, cache_mem_k, cache_mem_v, page_table, mem_prompt,
           g_mix, w_in, w_alpha_up, b_alpha, g_gla_out, w_out, g_mem_q, g_mem_kv, w_mem_q, w_mem_k, w_mem_v,
           w_mem_o, g_ffn, w_peer_q, peer_sub_keys, expert_u, expert_v, g_final):
    depth = w_in.shape[0]
    assert depth == 1, "the final norm is fused into the last layer's PEER kernel"
    b, s, _ = x_prompt.shape
    db, t, _ = x_sample.shape
    n_pages = page_table.shape[1]
    page = cache_k.shape[2]
    past = n_pages * page
    ppb = MOBA_BLOCK // page
    nfull = past // MOBA_BLOCK
    assert past % MOBA_BLOCK == 0 and t <= SAMPLE_TPAD and t <= MOBA_BLOCK and nfull >= MOBA_TOPK
    mem_len = mem_prompt.shape[1]

    l = 0
    wts = dict(
        g_gla_out=g_gla_out[l], w_out=w_out[l].astype(BF16), g_mem_q=g_mem_q[l],
        w_mem_q=w_mem_q[l].astype(BF16), w_mem_o=w_mem_o[l].astype(BF16), g_ffn=g_ffn[l],
        wq_hp=w_peer_q[l].astype(BF16).reshape(D_MODEL, PEER_HEADS * 2, PEER_DQ // 2).transpose(1, 0, 2),
        keys_hp=peer_sub_keys[l].astype(BF16).transpose(1, 0, 2, 3).reshape(
            PEER_HEADS * 2, PEER_NKEYS, PEER_DQ // 2),
        table=_pack_experts(expert_u[l], expert_v[l]),
    )
    w_in_p = jnp.pad(w_in[l], ((0, 0), (0, IN_COLS_PAD - IN_COLS))).astype(BF16)
    wa_p = jnp.pad(w_alpha_up[l], ((0, LANES - GLA_LOWRANK), (0, 0))).astype(BF16)

    xp = x_prompt.reshape(b * s, D_MODEL)
    q, k, v, qg, kg, vg, rg, la = _mixer_inputs(xp, jnp.arange(s), g_mix[l], w_in_p, wa_p, b_alpha[l], BF16)
    o_moba = _moba_prompt(q.reshape(b, s, MOBA_W), k.reshape(b, s, MOBA_W), v.reshape(b, s, MOBA_W))
    c = GLA_CHUNK if s % GLA_CHUNK == 0 else s
    o_gla, st_p = _gla(qg, kg, vg, la, jnp.zeros((b, GLA_HEADS, GLA_DK, GLA_DV), F32), b, s, c,
                       _row_tile(s), c)
    mk_p, mv_p = _mem_kv(mem_prompt.reshape(b * mem_len, D_MODEL), g_mem_kv[l],
                         w_mem_k[l].astype(BF16), w_mem_v[l].astype(BF16))
    y_p = _layer_tail(xp, o_moba.reshape(b * s, MOBA_W), o_gla, rg, b, s,
                      mk_p.reshape(b, mem_len, D_MODEL), mv_p.reshape(b, mem_len, D_MODEL), wts, None, g_final)

    tp = SAMPLE_TPAD
    xs = jnp.pad(x_sample, ((0, 0), (0, tp - t), (0, 0))).reshape(db * tp, D_MODEL)
    pos_s = past + (jnp.arange(db * tp) % tp)
    qs, ks, vs, qgs, kgs, vgs, rgs, las = _mixer_inputs(xs, pos_s, g_mix[l], w_in_p, wa_p, b_alpha[l], F32)
    ck3 = cache_k[l].reshape(cache_k.shape[1], page, MOBA_W)
    cv3 = cache_v[l].reshape(cache_v.shape[1], page, MOBA_W)
    sel = _sample_gate(page_table, _page_sums(ck3), qs, nfull, ppb)
    sel = sel.reshape(db, tp, LANES)[:, :t, :MOBA_HEADS * MOBA_TOPK].reshape(db, t * MOBA_HEADS * MOBA_TOPK)
    o_moba_s = _sample_attn(page_table, sel, qs, ks, vs, ck3, cv3, t, ppb)
    o_gla_s, st_s = _gla(qgs, kgs, vgs, las, state_gla[l], db, tp, tp, tp, t)
    mem_s = cache_mem_k.shape[2]
    y_s = _layer_tail(xs, o_moba_s, o_gla_s, rgs, db, tp,
                      cache_mem_k[l].reshape(db, mem_s, D_MODEL), cache_mem_v[l].reshape(db, mem_s, D_MODEL),
                      wts, t, g_final)

    heads = (MOBA_HEADS, MOBA_DH)
    k_s = ks.reshape(db, tp, *heads)[:, :t]
    v_s = vs.reshape(db, tp, *heads)[:, :t]
    return (y_p.reshape(b, s, D_MODEL), y_s.reshape(db, t, D_MODEL),
            k.reshape(1, b, s, *heads), v.reshape(1, b, s, *heads), st_p[None],
            mk_p.reshape(1, b, mem_len, MEM_HEADS, MEM_DH), mv_p.reshape(1, b, mem_len, MEM_HEADS, MEM_DH),
            k_s[None], v_s[None], st_s[None])
```

```python
import functools
import math

import numpy as np
import jax
import jax.numpy as jnp
from jax import lax
from jax.experimental import pallas as pl
from jax.experimental.pallas import tpu as pltpu

F32 = jnp.float32
BF16 = jnp.bfloat16
I32 = jnp.int32

EPS = 1e-6
NEG = -1e30
LANES = 128
SUBLANES = 8
VMEM_LIMIT = 56 * 1024 * 1024

D_MODEL = 1024
MOBA_HEADS, MOBA_DH, MOBA_BLOCK, MOBA_TOPK = 8, 64, 256, 3
MOBA_W = MOBA_HEADS * MOBA_DH
ROPE_THETA, ROPE_DIM = 500000.0, 16
GLA_HEADS, GLA_DK, GLA_DV = 4, 64, 128
GLA_KW, GLA_W = GLA_HEADS * GLA_DK, GLA_HEADS * GLA_DV
GLA_LOWRANK, GLA_TAU, GLA_CHUNK = 16, 16.0, 64
IN_COLS = 3 * MOBA_W + 2 * GLA_KW + 2 * GLA_W + GLA_LOWRANK
IN_COLS_PAD = 3 * MOBA_W + 2 * GLA_KW + 2 * GLA_W + LANES
MEM_HEADS, MEM_DH = 4, 256
PEER_HEADS, PEER_NKEYS, PEER_DQ, PEER_TOPK = 8, 128, 256, 16
PEER_PAIRS = PEER_HEADS * PEER_TOPK
SAMPLE_TPAD = 8


def _params(sem):
    return pltpu.CompilerParams(dimension_semantics=sem, vmem_limit_bytes=VMEM_LIMIT)


def _dot(a, b):
    return jnp.dot(a, b, preferred_element_type=F32)


def _dot_nt(a, b):
    return lax.dot_general(a, b, (((1,), (1,)), ((), ())), preferred_element_type=F32)


def _dot_tn(a, b):
    return lax.dot_general(a, b, (((0,), (0,)), ((), ())), preferred_element_type=F32)


def _split3(x):
    x1 = x.astype(BF16)
    r1 = x - x1.astype(F32)
    x2 = r1.astype(BF16)
    x3 = (r1 - x2.astype(F32)).astype(BF16)
    return x1, x2, x3


def _dot_exact_lhs(sel, x):
    a, b, c = _split3(x)
    return _dot(sel, a) + _dot(sel, b) + _dot(sel, c)


def _rms(x, g):
    return x * lax.rsqrt(jnp.mean(x * x, axis=-1, keepdims=True) + EPS) * g


def _row_tile(n):
    return 256 if n % 256 == 0 else n


def _proj_kernel(x_ref, g_ref, w_ref, wa_ref, ba_ref, cos_ref, sa_ref, sb_ref,
                 q_ref, k_ref, v_ref, qg_ref, kg_ref, vg_ref, rg_ref, la_ref):
    h = _rms(x_ref[...], g_ref[...]).astype(BF16)
    cos, sa, sb = cos_ref[...], sa_ref[...], sb_ref[...]

    def rope(z):
        outs = []
        for c in range(MOBA_W // LANES):
            zc = z[:, c * LANES:(c + 1) * LANES]
            outs.append(zc * cos + pltpu.roll(zc, ROPE_DIM // 2, 1) * sa
                        + pltpu.roll(zc, LANES - ROPE_DIM // 2, 1) * sb)
        return jnp.concatenate(outs, axis=1)

    o = 0
    q = rope(_dot(h, w_ref[:, o:o + MOBA_W])); o += MOBA_W
    q_ref[...] = (q * (MOBA_DH ** -0.5)).astype(q_ref.dtype)
    k_ref[...] = rope(_dot(h, w_ref[:, o:o + MOBA_W])); o += MOBA_W
    v_ref[...] = _dot(h, w_ref[:, o:o + MOBA_W]); o += MOBA_W
    qg_ref[...] = _dot(h, w_ref[:, o:o + GLA_KW]) * (GLA_DK ** -0.5); o += GLA_KW
    kg_ref[...] = _dot(h, w_ref[:, o:o + GLA_KW]); o += GLA_KW
    vg_ref[...] = _dot(h, w_ref[:, o:o + GLA_W]); o += GLA_W
    rg_ref[...] = _dot(h, w_ref[:, o:o + GLA_W]); o += GLA_W
    ag = _dot(h, w_ref[:, o:o + LANES])
    xg = _dot(ag.astype(BF16), wa_ref[...]) + ba_ref[...]
    log_sig = jnp.minimum(xg, 0.0) - jnp.log(1.0 + jnp.exp(-jnp.abs(xg)))
    la_ref[...] = log_sig * (1.0 / GLA_TAU)


def _rope_tables(pos):
    half = ROPE_DIM // 2
    inv_freq = 1.0 / (ROPE_THETA ** (jnp.arange(half, dtype=F32) * (2.0 / ROPE_DIM)))
    ang = pos.astype(F32)[:, None] * inv_freq[None, :]
    cos, sin = jnp.cos(ang), jnp.sin(ang)
    n = pos.shape[0]
    one = jnp.ones((n, MOBA_DH - ROPE_DIM), F32)
    zero = jnp.zeros((n, MOBA_DH - ROPE_DIM), F32)
    zh = jnp.zeros((n, half), F32)
    c64 = jnp.concatenate([cos, cos, one], axis=1)
    sa64 = jnp.concatenate([zh, sin, zero], axis=1)
    sb64 = jnp.concatenate([-sin, zh, zero], axis=1)
    rep = LANES // MOBA_DH
    return (jnp.tile(c64, (1, rep)), jnp.tile(sa64, (1, rep)), jnp.tile(sb64, (1, rep)))


def _mixer_inputs(x2d, pos_rows, g_mix, w_in_p, wa_p, b_alpha, q_dtype):
    n = x2d.shape[0]
    tm = _row_tile(n)
    nrep = pos_rows.shape[0] // tm
    cos, sa, sb = _rope_tables(pos_rows)
    row = lambda w: pl.BlockSpec((tm, w), lambda i: (i, 0))
    full = lambda a: pl.BlockSpec(a.shape, lambda i: (0, 0))
    tab = pl.BlockSpec((tm, LANES), lambda i: (i % nrep, 0))
    g2 = g_mix.reshape(1, D_MODEL)
    b2 = b_alpha.reshape(1, GLA_KW)
    widths = (MOBA_W, MOBA_W, MOBA_W, GLA_KW, GLA_KW, GLA_W, GLA_W, GLA_KW)
    dts = (q_dtype, F32, F32, F32, F32, F32, F32, F32)
    return pl.pallas_call(
        _proj_kernel,
        grid=(n // tm,),
        in_specs=[row(D_MODEL), full(g2), full(w_in_p), full(wa_p), full(b2), tab, tab, tab],
        out_specs=[row(w) for w in widths],
        out_shape=[jax.ShapeDtypeStruct((n, w), dt) for w, dt in zip(widths, dts)],
        compiler_params=_params(("arbitrary",)),
        name="mixer_inputs",
    )(x2d, g2, w_in_p, wa_p, b2, cos, sa, sb)


def _moba_prompt_kernel(q_ref, k_ref, v_ref, o_ref, km_ref, m_ref, l_ref, acc_ref, *, nblk):
    i = pl.program_id(2)
    blk = MOBA_BLOCK
    hpl = LANES // MOBA_DH

    @pl.when(i == 0)
    def _():
        for j in range(nblk):
            km_ref[j:j + 1, :] = jnp.mean(k_ref[0, j * blk:(j + 1) * blk, :], axis=0, keepdims=True)

    q = q_ref[0]
    jrow = lax.broadcasted_iota(I32, (nblk, blk), 0)
    rowi = lax.broadcasted_iota(I32, (blk, blk), 0)
    coli = lax.broadcasted_iota(I32, (blk, blk), 1)
    lane_q = lax.broadcasted_iota(I32, (blk, LANES), 1)
    lane_m = lax.broadcasted_iota(I32, (nblk, LANES), 1)
    blk_m = lax.broadcasted_iota(I32, (nblk, LANES), 0)
    off = pl.multiple_of(i * blk, blk)
    km = km_ref[...]
    q_aug, own, onehot = [], [], []
    for hh in range(hpl):
        own_q = (lane_q // MOBA_DH) == hh
        own.append(own_q)
        k1, k2, k3 = _split3(jnp.where((lane_m // MOBA_DH) == hh, km, 0.0))
        gt = _dot_nt(k1, q) + _dot_nt(k2, q) + _dot_nt(k3, q)
        gm = jnp.where(jrow < i, gt, NEG)
        cnt = jnp.zeros((nblk, blk), I32)
        for jp in range(nblk):
            gj = gm[jp:jp + 1, :]
            beats = (gj > gm) | ((gj == gm) & (jp < jrow))
            cnt = cnt + beats.astype(I32)
        picked = (cnt < MOBA_TOPK) & (jrow < i)
        pen_t = jnp.where(picked, 0.0, NEG).astype(BF16)
        place = (lane_m == blk_m + (hpl - 1 - hh) * MOBA_DH).astype(BF16)
        onehot.append(place)
        pen = _dot_tn(pen_t, place).astype(BF16)
        q_aug.append(jnp.where(own_q, q, pen))

    def widen(x):
        return jnp.concatenate([x, x], axis=1)

    kd = k_ref[0, pl.ds(off, blk), :].astype(BF16)
    vd = v_ref[0, pl.ds(off, blk), :].astype(BF16)
    for hh in range(hpl):
        s = _dot_nt(q_aug[hh], jnp.where(own[hh], kd, jnp.zeros_like(kd)))
        s = jnp.where(coli <= rowi, s, NEG)
        m0 = jnp.max(s, axis=1, keepdims=True)
        p = jnp.exp(s - m0)
        m_ref[hh] = jnp.broadcast_to(m0, (blk, LANES))
        l_ref[hh] = jnp.broadcast_to(jnp.sum(p, axis=1, keepdims=True), (blk, LANES))
        acc_ref[hh] = _dot(p.astype(BF16), vd)

    for j in range(nblk - 1):
        @pl.when(j < i)
        def _(j=j):
            kj = k_ref[0, j * blk:(j + 1) * blk, :].astype(BF16)
            vj = v_ref[0, j * blk:(j + 1) * blk, :].astype(BF16)
            for hh in range(hpl):
                k_aug = jnp.where(own[hh], kj, jnp.broadcast_to(onehot[hh][j:j + 1, :], (blk, LANES)))
                sj = _dot_nt(q_aug[hh], k_aug)
                m_old = m_ref[hh]
                m_new = jnp.maximum(m_old, jnp.max(sj, axis=1, keepdims=True))
                alpha = jnp.exp(m_old - m_new)
                pj = jnp.exp(sj - widen(m_new))
                l_ref[hh] = alpha * l_ref[hh] + jnp.sum(pj, axis=1, keepdims=True)
                acc_ref[hh] = alpha * acc_ref[hh] + _dot(pj.astype(BF16), vj)
                m_ref[hh] = m_new

    o_ref[0] = jnp.where(own[0], acc_ref[0] / l_ref[0], acc_ref[1] / l_ref[1])


def _moba_prompt(q, k, v):
    b, s, w = q.shape
    assert s % MOBA_BLOCK == 0 and w == MOBA_W
    nblk = s // MOBA_BLOCK
    assert nblk % SUBLANES == 0 or nblk < SUBLANES
    hpl = LANES // MOBA_DH
    assert hpl == 2
    kv_spec = pl.BlockSpec((1, s, LANES), lambda bi, hp, i: (bi, 0, hp))
    q_spec = pl.BlockSpec((1, MOBA_BLOCK, LANES), lambda bi, hp, i: (bi, i, hp))
    return pl.pallas_call(
        functools.partial(_moba_prompt_kernel, nblk=nblk),
        grid=(b, w // LANES, nblk),
        in_specs=[q_spec, kv_spec, kv_spec],
        out_specs=q_spec,
        out_shape=jax.ShapeDtypeStruct((b, s, w), F32),
        scratch_shapes=[pltpu.VMEM((nblk, LANES), F32)]
        + [pltpu.VMEM((hpl, MOBA_BLOCK, LANES), F32)] * 3,
        compiler_params=_params(("arbitrary", "arbitrary", "arbitrary")),
        name="moba_prompt",
    )(q, k, v)


def _gla_consts(c):
    nlev = int(math.log2(c))
    assert 2 ** nlev == c
    r = np.arange(c)
    tri = (r[:, None] >= r[None, :]).astype(np.float32)
    sel = np.zeros((nlev, c, c), np.float32)
    msk = np.zeros((nlev + 1, c, c), np.float32)
    for l in range(nlev):
        b = 2 ** l
        ref_row = (r // (2 * b)) * (2 * b) + b - 1
        sel[l, r, ref_row] = 1.0
        same = (r[:, None] // (2 * b)) == (r[None, :] // (2 * b))
        msk[l] = same & ((r[:, None] % (2 * b)) >= b) & ((r[None, :] % (2 * b)) < b)
    msk[nlev] = np.eye(c)
    return jnp.asarray(tri, BF16), jnp.asarray(sel, BF16), jnp.asarray(msk, F32)


def _gla_kernel(q_ref, k_ref, v_ref, la_ref, s0_ref, tri_ref, sel_ref, msk_ref, o_ref, so_ref, st_ref,
                *, c, nchunk, nlev, t_real, nsteps):
    step = pl.program_id(1)

    @pl.when(step == 0)
    def _():
        st_ref[...] = s0_ref[0]

    eye = (lax.broadcasted_iota(I32, (GLA_DK, GLA_DK), 0)
           == lax.broadcasted_iota(I32, (GLA_DK, GLA_DK), 1)).astype(F32)
    for ci in range(nchunk):
        rows = slice(ci * c, (ci + 1) * c)
        q, k, v, la = q_ref[rows, :], k_ref[rows, :], v_ref[rows, :], la_ref[rows, :]
        if t_real < c:
            live = lax.broadcasted_iota(I32, (c, 1), 0) < t_real
            la = jnp.where(live, la, 0.0)
            k = jnp.where(live, k, 0.0)
        a = _dot_exact_lhs(tri_ref[...], la)
        a_last = a[c - 1:c, :]
        qe = (q * jnp.exp(a)).astype(BF16)
        kd = (k * jnp.exp(a_last - a)).astype(BF16)
        e_last = jnp.exp(a_last)
        qb, kb, vb = q.astype(BF16), k.astype(BF16), v.astype(BF16)
        qt, kt = [], []
        for l in range(nlev):
            aref = _dot_exact_lhs(sel_ref[l], a)
            qt.append((q * jnp.exp(jnp.minimum(a - aref, 0.0))).astype(BF16))
            kt.append((k * jnp.exp(jnp.minimum(aref - a, 0.0))).astype(BF16))
        outs = []
        for h in range(GLA_HEADS):
            hs = slice(h * GLA_DK, (h + 1) * GLA_DK)
            vs = slice(h * GLA_DV, (h + 1) * GLA_DV)
            att = msk_ref[nlev] * _dot_nt(qb[:, hs], kb[:, hs])
            for l in range(nlev):
                att = att + msk_ref[l] * _dot_nt(qt[l][:, hs], kt[l][:, hs])
            sh = st_ref[h]
            outs.append(_dot(att.astype(BF16), vb[:, vs]) + _dot(qe[:, hs], sh.astype(BF16)))
            e_col = jnp.sum(eye * e_last[:, hs], axis=1, keepdims=True)
            st_ref[h] = e_col * sh + _dot_tn(kd[:, hs], vb[:, vs])
        o_ref[rows, :] = jnp.concatenate(outs, axis=1)

    @pl.when(step == nsteps - 1)
    def _():
        so_ref[0] = st_ref[...]


def _gla(qg, kg, vg, la, s0, b, t, c, rows_per_step, t_real):
    assert t % rows_per_step == 0 and rows_per_step % c == 0
    nsteps = t // rows_per_step
    nlev = int(math.log2(c))
    tri, sel, msk = _gla_consts(c)
    row = lambda w: pl.BlockSpec((rows_per_step, w), lambda bi, i: (bi * nsteps + i, 0))
    st = pl.BlockSpec((1, GLA_HEADS, GLA_DK, GLA_DV), lambda bi, i: (bi, 0, 0, 0))
    const2 = pl.BlockSpec(tri.shape, lambda bi, i: (0, 0))
    const3 = lambda a: pl.BlockSpec(a.shape, lambda bi, i: (0, 0, 0))
    return pl.pallas_call(
        functools.partial(_gla_kernel, c=c, nchunk=rows_per_step // c, nlev=nlev, t_real=t_real,
                          nsteps=nsteps),
        grid=(b, nsteps),
        in_specs=[row(GLA_KW), row(GLA_KW), row(GLA_W), row(GLA_KW), st, const2, const3(sel), const3(msk)],
        out_specs=[row(GLA_W), st],
        out_shape=[jax.ShapeDtypeStruct((b * t, GLA_W), F32),
                   jax.ShapeDtypeStruct((b, GLA_HEADS, GLA_DK, GLA_DV), F32)],
        scratch_shapes=[pltpu.VMEM((GLA_HEADS, GLA_DK, GLA_DV), F32)],
        compiler_params=_params(("arbitrary", "arbitrary")),
        name="gla",
    )(qg, kg, vg, la, s0, tri, sel, msk)


def _mixer_out_kernel(x_ref, om_ref, og_ref, rg_ref, gg_ref, wo_ref, gq_ref, wq_ref, x1_ref, qm_ref):
    og = og_ref[...]
    parts = []
    for h in range(GLA_HEADS):
        oh = og[:, h * GLA_DV:(h + 1) * GLA_DV]
        parts.append(_rms(oh, gg_ref[...]))
    rg = rg_ref[...]
    ogn = jnp.concatenate(parts, axis=1) * (rg * jax.nn.sigmoid(rg))
    cat = jnp.concatenate([om_ref[...], ogn], axis=1).astype(BF16)
    x1 = x_ref[...] + _dot(cat, wo_ref[...])
    x1_ref[...] = x1
    qm_ref[...] = _dot(_rms(x1, gq_ref[...]).astype(BF16), wq_ref[...]).astype(qm_ref.dtype)


def _mixer_out(x2d, o_moba, o_gla, rg, g_gla_out, w_out, g_mem_q, w_mem_q):
    n = x2d.shape[0]
    tm = _row_tile(n)
    row = lambda w: pl.BlockSpec((tm, w), lambda i: (i, 0))
    full = lambda a: pl.BlockSpec(a.shape, lambda i: (0, 0))
    gg = g_gla_out.reshape(1, GLA_DV)
    gq = g_mem_q.reshape(1, D_MODEL)
    return pl.pallas_call(
        _mixer_out_kernel,
        grid=(n // tm,),
        in_specs=[row(D_MODEL), row(MOBA_W), row(GLA_W), row(GLA_W), full(gg), full(w_out), full(gq),
                  full(w_mem_q)],
        out_specs=[row(D_MODEL), row(D_MODEL)],
        out_shape=[jax.ShapeDtypeStruct((n, D_MODEL), F32), jax.ShapeDtypeStruct((n, D_MODEL), BF16)],
        compiler_params=_params(("arbitrary",)),
        name="mixer_out",
    )(x2d, o_moba, o_gla, rg, gg, w_out, gq, w_mem_q)


def _mem_kv_kernel(m_ref, g_ref, wk_ref, wv_ref, mk_ref, mv_ref):
    h = _rms(m_ref[...], g_ref[...]).astype(BF16)
    mk_ref[...] = _dot(h, wk_ref[...])
    mv_ref[...] = _dot(h, wv_ref[...])


def _mem_kv(mem2d, g_mem_kv, w_mem_k, w_mem_v):
    n = mem2d.shape[0]
    tm = _row_tile(n)
    row = pl.BlockSpec((tm, D_MODEL), lambda i: (i, 0))
    full = lambda a: pl.BlockSpec(a.shape, lambda i: (0, 0))
    g = g_mem_kv.reshape(1, D_MODEL)
    return pl.pallas_call(
        _mem_kv_kernel,
        grid=(n // tm,),
        in_specs=[row, full(g), full(w_mem_k), full(w_mem_v)],
        out_specs=[row, row],
        out_shape=[jax.ShapeDtypeStruct((n, D_MODEL), F32)] * 2,
        compiler_params=_params(("arbitrary",)),
        name="memory_kv",
    )(mem2d, g, w_mem_k, w_mem_v)


def _mem_attend_kernel(q_ref, mk_ref, mv_ref, x1_ref, wo_ref, gf_ref, x2_ref, h3_ref):
    q = q_ref[...]
    mk = mk_ref[0].astype(BF16)
    mv = mv_ref[0].astype(BF16)
    outs = []
    for h in range(MEM_HEADS):
        hs = slice(h * MEM_DH, (h + 1) * MEM_DH)
        s = _dot_nt(q[:, hs], mk[:, hs]) * (MEM_DH ** -0.5)
        p = jnp.exp(s - jnp.max(s, axis=1, keepdims=True))
        l = jnp.sum(p, axis=1, keepdims=True)
        outs.append(_dot(p.astype(BF16), mv[:, hs]) / l)
    o = jnp.concatenate(outs, axis=1).astype(BF16)
    x2 = x1_ref[...] + _dot(o, wo_ref[...])
    x2_ref[...] = x2
    h3_ref[...] = _rms(x2, gf_ref[...])


def _mem_attend(qmem, mk, mv, x1, w_mem_o, g_ffn, b, t):
    tq = _row_tile(t)
    nq = t // tq
    row = pl.BlockSpec((tq, D_MODEL), lambda bi, i: (bi * nq + i, 0))
    mem = pl.BlockSpec((1,) + mk.shape[1:], lambda bi, i: (bi, 0, 0))
    full = lambda a: pl.BlockSpec(a.shape, lambda bi, i: (0, 0))
    gf = g_ffn.reshape(1, D_MODEL)
    return pl.pallas_call(
        _mem_attend_kernel,
        grid=(b, nq),
        in_specs=[row, mem, mem, row, full(w_mem_o), full(gf)],
        out_specs=[row, row],
        out_shape=[jax.ShapeDtypeStruct((b * t, D_MODEL), F32)] * 2,
        compiler_params=_params(("arbitrary", "arbitrary")),
        name="memory_attend",
    )(qmem, mk, mv, x1, w_mem_o, gf)


def _peer_candidates():
    k = PEER_TOPK
    flat = []
    flat += [0 * k + j for j in range(k)]
    for i in range(1, 8):
        flat += [i * k + j if (i + 1) * (j + 1) <= k else -1 for j in range(8)]
    flat += [(8 + r) * k for r in range(8)]
    return np.asarray(flat, np.int32)


def _route_kernel(h_ref, wq_ref, keys_ref, flat_ref, eid_ref, g_ref):
    tq = h_ref.shape[0]
    k = PEER_TOPK
    hb = h_ref[...].astype(BF16)
    kio = lax.broadcasted_iota(I32, (PEER_NKEYS, tq), 0)
    r16 = lax.broadcasted_iota(I32, (k, tq), 0)
    flat = flat_ref[...]
    nexp = PEER_NKEYS * PEER_NKEYS
    big = jnp.int32(2 ** 30)
    qv_all = _dot(hb, wq_ref[...]).astype(BF16)

    def top16(s):
        def body(kk, carry):
            s, vals, idxs = carry
            m = jnp.max(s, axis=0, keepdims=True)
            idx = jnp.min(jnp.where(s == m, kio, PEER_NKEYS), axis=0, keepdims=True)
            vals = jnp.where(r16 == kk, m, vals)
            idxs = jnp.where(r16 == kk, idx, idxs)
            s = jnp.where(kio == idx, -jnp.inf, s)
            return s, vals, idxs
        _, vals, idxs = lax.fori_loop(0, k, body, (s, jnp.zeros((k, tq), F32), jnp.zeros((k, tq), I32)))
        return vals, idxs

    for h in range(PEER_HEADS):
        tv, ti = [], []
        for p in range(2):
            hp = h * 2 + p
            qv = qv_all[:, hp * (PEER_DQ // 2):(hp + 1) * (PEER_DQ // 2)]
            s = _dot_nt(keys_ref[hp], qv)
            vals, idxs = top16(s)
            tv.append(vals)
            ti.append(idxs)
        a, b = tv
        ia, ib = ti
        cs = [a[0:1] + b]
        ce = [ia[0:1] * PEER_NKEYS + ib]
        for i in range(1, 8):
            cs.append(a[i:i + 1] + b[0:8])
            ce.append(ia[i:i + 1] * PEER_NKEYS + ib[0:8])
        cs.append(a[8:16] + b[0:1])
        ce.append(ia[8:16] * PEER_NKEYS + ib[0:1])
        cand = jnp.where(flat >= 0, jnp.concatenate(cs, axis=0), -jnp.inf)
        ckey = jnp.where(flat >= 0, flat * nexp + jnp.concatenate(ce, axis=0), big)

        def body2(kk, carry):
            cand, vals, eids = carry
            m = jnp.max(cand, axis=0, keepdims=True)
            pick = jnp.min(jnp.where(cand == m, ckey, big), axis=0, keepdims=True)
            vals = jnp.where(r16 == kk, m, vals)
            eids = jnp.where(r16 == kk, pick & (nexp - 1), eids)
            cand = jnp.where(ckey == pick, -jnp.inf, cand)
            return cand, vals, eids
        _, fin, eids = lax.fori_loop(0, k, body2, (cand, jnp.zeros((k, tq), F32), jnp.zeros((k, tq), I32)))
        ex = jnp.exp(fin - fin[0:1])
        g_ref[h * k:(h + 1) * k, :] = ex / jnp.sum(ex, axis=0, keepdims=True)
        eid_ref[h * k:(h + 1) * k, :] = eids


def _route(h3, wq_hp, keys_hp):
    n = h3.shape[0]
    tq = _row_tile(n)
    flat = jnp.broadcast_to(jnp.asarray(_peer_candidates())[:, None], (_peer_candidates().shape[0], tq))
    out = pl.BlockSpec((PEER_PAIRS, tq), lambda i: (0, i))
    return pl.pallas_call(
        _route_kernel,
        grid=(n // tq,),
        in_specs=[pl.BlockSpec((tq, D_MODEL), lambda i: (i, 0)),
                  pl.BlockSpec(wq_hp.shape, lambda i: (0, 0)),
                  pl.BlockSpec(keys_hp.shape, lambda i: (0, 0, 0)),
                  pl.BlockSpec(flat.shape, lambda i: (0, 0))],
        out_specs=[out, out],
        out_shape=[jax.ShapeDtypeStruct((PEER_PAIRS, n), I32), jax.ShapeDtypeStruct((PEER_PAIRS, n), F32)],
        compiler_params=_params(("arbitrary",)),
        name="peer_route",
    )(h3, wq_hp, keys_hp, flat)


PEER_TT = 8
ROW_WORDS = D_MODEL // 2
ROW_SUB = ROW_WORDS // LANES
REC_SUB = 2 * ROW_SUB
assert REC_SUB == SUBLANES


def _pack_experts(u, v):
    def words(t):
        tb = lax.bitcast_convert_type(t.astype(BF16), jnp.uint16).astype(jnp.uint32)
        w = tb[:, :ROW_WORDS] | (tb[:, ROW_WORDS:] << 16)
        return w.reshape(t.shape[0], ROW_SUB, LANES)
    tab = jnp.concatenate([words(u), words(v)], axis=1)
    return lax.bitcast_convert_type(tab, I32).reshape(u.shape[0] * REC_SUB, LANES)


def _unpack(w):
    lo = lax.bitcast_convert_type(w << 16, F32)
    hi = lax.bitcast_convert_type(w & jnp.int32(-65536), F32)
    return lo, hi


def _peer_kernel(eid_ref, eidn_ref, g_ref, h_ref, x2_ref, gf_ref, tab_ref, y_ref, buf_a, buf_b, sem, po_ref,
                 *, nsteps):
    i = pl.program_id(0)
    tile_rows = PEER_TT * PEER_PAIRS * REC_SUB

    def gather(eref, row0, buf, s, t):
        for p in range(PEER_PAIRS):
            src = pl.multiple_of(eref[row0 + t, p] * REC_SUB, REC_SUB)
            dst = pl.multiple_of((t * PEER_PAIRS + p) * REC_SUB, REC_SUB)
            pltpu.make_async_copy(tab_ref.at[pl.ds(src, REC_SUB), :],
                                  buf.at[pl.ds(dst, REC_SUB), :], sem.at[s]).start(priority=p % 2)

    def wait_tile(buf, s):
        pltpu.make_async_copy(tab_ref.at[pl.ds(0, tile_rows), :], buf, sem.at[s]).wait()

    eye = (lax.broadcasted_iota(I32, (PEER_PAIRS, PEER_PAIRS), 0)
           == lax.broadcasted_iota(I32, (PEER_PAIRS, PEER_PAIRS), 1)).astype(F32)

    def consume(buf, row0, t):
        base = t * (PEER_PAIRS * REC_SUB)
        x = h_ref[pl.ds(row0 + t, 1), :]
        dacc = jnp.zeros((PEER_PAIRS, LANES), F32)
        for s in range(ROW_SUB):
            lo, hi = _unpack(buf[pl.ds(base + s, PEER_PAIRS, stride=REC_SUB), :])
            dacc = dacc + lo * x[:, s * LANES:(s + 1) * LANES] \
                + hi * x[:, ROW_WORDS + s * LANES:ROW_WORDS + (s + 1) * LANES]
        d = jnp.sum(dacc, axis=1, keepdims=True)
        act = 0.5 * d * (1.0 + lax.erf(d * (2.0 ** -0.5)))
        gcol = jnp.sum(eye * g_ref[pl.ds(row0 + t, 1), :], axis=1, keepdims=True)
        wcol = gcol * act
        los, his = [], []
        for s in range(ROW_SUB, REC_SUB):
            lo, hi = _unpack(buf[pl.ds(base + s, PEER_PAIRS, stride=REC_SUB), :])
            los.append(jnp.sum(wcol * lo, axis=0, keepdims=True))
            his.append(jnp.sum(wcol * hi, axis=0, keepdims=True))
        po_ref[pl.ds(row0 + t, 1), :] = jnp.concatenate(los + his, axis=1)

    @pl.when(i == 0)
    def _():
        def first(t, c):
            gather(eid_ref, 0, buf_a, 0, t)
            return c
        lax.fori_loop(0, PEER_TT, first, 0)

    def phase_a(t, c):
        gather(eid_ref, PEER_TT, buf_b, 1, t)
        consume(buf_a, 0, t)
        return c

    def phase_b(t, c, prefetch):
        if prefetch:
            gather(eidn_ref, 0, buf_a, 0, t)
        consume(buf_b, PEER_TT, t)
        return c

    wait_tile(buf_a, 0)
    lax.fori_loop(0, PEER_TT, phase_a, 0)
    wait_tile(buf_b, 1)
    n_prefetch = jnp.where(i + 1 < nsteps, PEER_TT, 0)
    lax.fori_loop(0, n_prefetch, functools.partial(phase_b, prefetch=True), 0)
    lax.fori_loop(n_prefetch, PEER_TT, functools.partial(phase_b, prefetch=False), 0)
    y_ref[...] = _rms(x2_ref[...] + po_ref[...], gf_ref[...])


def _peer_mix(eid, g, h3, x2, g_final, table):
    n = h3.shape[0]
    rows = 2 * PEER_TT
    assert n % rows == 0
    nsteps = n // rows
    row = lambda w: pl.BlockSpec((rows, w), lambda i: (i, 0))
    gf = g_final.reshape(1, D_MODEL)
    buf = pltpu.VMEM((PEER_TT * PEER_PAIRS * REC_SUB, LANES), I32)
    return pl.pallas_call(
        functools.partial(_peer_kernel, nsteps=nsteps),
        grid=(nsteps,),
        in_specs=[pl.BlockSpec((rows, PEER_PAIRS), lambda i: (i, 0), memory_space=pltpu.SMEM),
                  pl.BlockSpec((rows, PEER_PAIRS), lambda i: (jnp.minimum(i + 1, nsteps - 1), 0),
                               memory_space=pltpu.SMEM),
                  row(PEER_PAIRS), row(D_MODEL), row(D_MODEL),
                  pl.BlockSpec((1, D_MODEL), lambda i: (0, 0)),
                  pl.BlockSpec(memory_space=pl.ANY)],
        out_specs=row(D_MODEL),
        out_shape=jax.ShapeDtypeStruct((n, D_MODEL), F32),
        scratch_shapes=[buf, buf, pltpu.SemaphoreType.DMA((2,)), pltpu.VMEM((rows, D_MODEL), F32)],
        compiler_params=_params(("arbitrary",)),
        name="peer_mix",
    )(eid, eid, g, h3, x2, gf, table)


def _page_sum_kernel(k_ref, o_ref):
    for pg in range(k_ref.shape[0]):
        o_ref[pg:pg + 1, :] = jnp.sum(k_ref[pg], axis=0, keepdims=True)


def _page_sums(cache_k3):
    p, ps, w = cache_k3.shape
    pb = max(d for d in range(1, 33) if p % d == 0 and (d % SUBLANES == 0 or d == p))
    return pl.pallas_call(
        _page_sum_kernel,
        grid=(p // pb,),
        in_specs=[pl.BlockSpec((pb, ps, w), lambda i: (i, 0, 0))],
        out_specs=pl.BlockSpec((pb, w), lambda i: (i, 0)),
        out_shape=jax.ShapeDtypeStruct((p, w), F32),
        compiler_params=_params(("arbitrary",)),
        name="page_sums",
    )(cache_k3)


def _sample_gate_kernel(pt_ref, ps_ref, q_ref, sel_ref, km_ref, *, nfull, ppb):
    b = pl.program_id(0)
    for j in range(nfull):
        acc = ps_ref[pl.ds(pt_ref[b, j * ppb], 1), :]
        for pp in range(1, ppb):
            acc = acc + ps_ref[pl.ds(pt_ref[b, j * ppb + pp], 1), :]
        km_ref[j:j + 1, :] = acc * (1.0 / MOBA_BLOCK)
    q = q_ref[...].astype(BF16)
    lane = lax.broadcasted_iota(I32, (SAMPLE_TPAD, nfull), 1)
    olane = lax.broadcasted_iota(I32, (SAMPLE_TPAD, LANES), 1)
    out = jnp.zeros((SAMPLE_TPAD, LANES), I32)
    for h in range(MOBA_HEADS):
        ls = slice(h * MOBA_DH, (h + 1) * MOBA_DH)
        k1, k2, k3 = _split3(km_ref[:, ls])
        qh = q[:, ls]
        g = _dot_nt(qh, k1) + _dot_nt(qh, k2) + _dot_nt(qh, k3)
        for kk in range(MOBA_TOPK):
            m = jnp.max(g, axis=1, keepdims=True)
            idx = jnp.min(jnp.where(g == m, lane, nfull), axis=1, keepdims=True)
            out = out + jnp.where(olane == h * MOBA_TOPK + kk, idx, 0)
            g = jnp.where(lane == idx, -jnp.inf, g)
    sel_ref[...] = out


def _sample_gate(page_table, page_sum, q_s, nfull, ppb):
    db = page_table.shape[0]
    return pl.pallas_call(
        functools.partial(_sample_gate_kernel, nfull=nfull, ppb=ppb),
        grid_spec=pltpu.PrefetchScalarGridSpec(
            num_scalar_prefetch=1,
            grid=(db,),
            in_specs=[pl.BlockSpec(page_sum.shape, lambda b, pt: (0, 0)),
                      pl.BlockSpec((SAMPLE_TPAD, MOBA_W), lambda b, pt: (b, 0))],
            out_specs=pl.BlockSpec((SAMPLE_TPAD, LANES), lambda b, pt: (b, 0)),
            scratch_shapes=[pltpu.VMEM((nfull, MOBA_W), F32)]),
        out_shape=jax.ShapeDtypeStruct((db * SAMPLE_TPAD, LANES), I32),
        compiler_params=_params(("arbitrary",)),
        name="sample_gate",
    )(page_table, page_sum, q_s)


def _sample_attn_kernel(pt_ref, sel_ref, q_ref, kn_ref, vn_ref, ck_ref, cv_ref, o_ref, kbuf, vbuf, sem,
                        *, t_real, ppb, page):
    b = pl.program_id(0)
    hp = pl.program_id(1)
    hpl = LANES // MOBA_DH
    nslab = t_real * hpl * MOBA_TOPK * ppb
    lane0 = pl.multiple_of(hp * LANES, LANES)

    def copies(n):
        t, r = divmod(n, hpl * MOBA_TOPK * ppb)
        hh, r = divmod(r, MOBA_TOPK * ppb)
        kk, pp = divmod(r, ppb)
        blk = sel_ref[b, (t * MOBA_HEADS + hp * hpl + hh) * MOBA_TOPK + kk]
        phys = pt_ref[b, blk * ppb + pp]
        return (pltpu.make_async_copy(ck_ref.at[phys, :, pl.ds(lane0, LANES)], kbuf.at[n], sem.at[0]),
                pltpu.make_async_copy(cv_ref.at[phys, :, pl.ds(lane0, LANES)], vbuf.at[n], sem.at[1]))

    for n in range(nslab):
        ck, cv = copies(n)
        ck.start()
        cv.start()
    for n in range(nslab):
        ck, cv = copies(n)
        ck.wait()
        cv.wait()

    q = q_ref[...].astype(BF16)
    kn = kn_ref[...].astype(BF16)
    vn = vn_ref[...].astype(BF16)
    rows = lax.broadcasted_iota(I32, (SAMPLE_TPAD, SAMPLE_TPAD), 0)
    cols = lax.broadcasted_iota(I32, (SAMPLE_TPAD, SAMPLE_TPAD), 1)
    rowc = lax.broadcasted_iota(I32, (SAMPLE_TPAD, 1), 0)
    nsel = MOBA_TOPK * ppb
    outs = []
    for hh in range(hpl):
        ls = slice(hh * MOBA_DH, (hh + 1) * MOBA_DH)
        qh = q[:, ls]
        s_new = jnp.where((cols <= rows) & (cols < t_real), _dot_nt(qh, kn[:, ls]), NEG)
        m_new = jnp.max(s_new, axis=1, keepdims=True)
        acc = jnp.zeros((SAMPLE_TPAD, MOBA_DH), F32)
        for t in range(t_real):
            n0 = (t * hpl + hh) * nsel
            kc = jnp.concatenate([kbuf[n0 + r][:, ls] for r in range(nsel)], axis=0).astype(BF16)
            vc = jnp.concatenate([vbuf[n0 + r][:, ls] for r in range(nsel)], axis=0).astype(BF16)
            s = _dot_nt(qh, kc)
            m = jnp.maximum(jnp.max(s, axis=1, keepdims=True), m_new)
            p = jnp.exp(s - m)
            pn = jnp.exp(s_new - m)
            l = jnp.sum(p, axis=1, keepdims=True) + jnp.sum(pn, axis=1, keepdims=True)
            o = (_dot(p.astype(BF16), vc) + _dot(pn.astype(BF16), vn[:, ls])) / l
            acc = acc + jnp.where(rowc == t, o, 0.0)
        outs.append(acc)
    o_ref[...] = jnp.concatenate(outs, axis=1)


def _sample_attn(page_table, sel, q_s, k_new, v_new, cache_k3, cache_v3, t_real, ppb):
    db = page_table.shape[0]
    page = cache_k3.shape[1]
    hpl = LANES // MOBA_DH
    nslab = t_real * hpl * MOBA_TOPK * ppb
    row = pl.BlockSpec((SAMPLE_TPAD, LANES), lambda b, hp, pt, sl: (b, hp))
    anyspec = pl.BlockSpec(memory_space=pl.ANY)
    return pl.pallas_call(
        functools.partial(_sample_attn_kernel, t_real=t_real, ppb=ppb, page=page),
        grid_spec=pltpu.PrefetchScalarGridSpec(
            num_scalar_prefetch=2,
            grid=(db, MOBA_W // LANES),
            in_specs=[row, row, row, anyspec, anyspec],
            out_specs=row,
            scratch_shapes=[pltpu.VMEM((nslab, page, LANES), F32),
                            pltpu.VMEM((nslab, page, LANES), F32),
                            pltpu.SemaphoreType.DMA((2,))]),
        out_shape=jax.ShapeDtypeStruct((db * SAMPLE_TPAD, MOBA_W), F32),
        compiler_params=_params(("arbitrary", "arbitrary")),
        name="sample_attn",
    )(page_table, sel, q_s, k_new, v_new, cache_k3, cache_v3)


def _layer_tail(x2d, o_moba, o_gla, rg, b, t, mk, mv, wts, live_rows, g_last):
    x1, qmem = _mixer_out(x2d, o_moba, o_gla, rg, wts["g_gla_out"], wts["w_out"], wts["g_mem_q"], wts["w_mem_q"])
    x2, h3 = _mem_attend(qmem, mk, mv, x1, wts["w_mem_o"], wts["g_ffn"], b, t)
    if live_rows is not None:
        x2 = x2.reshape(b, t, D_MODEL)[:, :live_rows].reshape(b * live_rows, D_MODEL)
        h3 = h3.reshape(b, t, D_MODEL)[:, :live_rows].reshape(b * live_rows, D_MODEL)
    eid_t, g_t = _route(h3, wts["wq_hp"], wts["keys_hp"])
    return _peer_mix(eid_t.T, g_t.T, h3, x2, g_last, wts["table"])


def kernel(x_prompt, x_sample, cache_k, cache_v, state_gla, cache_mem_k, cache_mem_v, page_table, mem_prompt,
           g_mix, w_in, w_alpha_up, b_alpha, g_gla_out, w_out, g_mem_q, g_mem_kv, w_mem_q, w_mem_k, w_mem_v,
           w_mem_o, g_ffn, w_peer_q, peer_sub_keys, expert_u, expert_v, g_final):
    depth = w_in.shape[0]
    assert depth == 1, "the final norm is fused into the last layer's PEER kernel"
    b, s, _ = x_prompt.shape
    db, t, _ = x_sample.shape
    n_pages = page_table.shape[1]
    page = cache_k.shape[2]
    past = n_pages * page
    ppb = MOBA_BLOCK // page
    nfull = past // MOBA_BLOCK
    assert past % MOBA_BLOCK == 0 and t <= SAMPLE_TPAD and t <= MOBA_BLOCK and nfull >= MOBA_TOPK
    mem_len = mem_prompt.shape[1]

    l = 0
    wts = dict(
        g_gla_out=g_gla_out[l], w_out=w_out[l].astype(BF16), g_mem_q=g_mem_q[l],
        w_mem_q=w_mem_q[l].astype(BF16), w_mem_o=w_mem_o[l].astype(BF16), g_ffn=g_ffn[l],
        wq_hp=w_peer_q[l].astype(BF16),
        keys_hp=peer_sub_keys[l].astype(BF16).transpose(1, 0, 2, 3).reshape(
            PEER_HEADS * 2, PEER_NKEYS, PEER_DQ // 2),
        table=_pack_experts(expert_u[l], expert_v[l]),
    )
    w_in_p = jnp.pad(w_in[l], ((0, 0), (0, IN_COLS_PAD - IN_COLS))).astype(BF16)
    wa_p = jnp.pad(w_alpha_up[l], ((0, LANES - GLA_LOWRANK), (0, 0))).astype(BF16)

    xp = x_prompt.reshape(b * s, D_MODEL)
    q, k, v, qg, kg, vg, rg, la = _mixer_inputs(xp, jnp.arange(s), g_mix[l], w_in_p, wa_p, b_alpha[l], BF16)
    o_moba = _moba_prompt(q.reshape(b, s, MOBA_W), k.reshape(b, s, MOBA_W), v.reshape(b, s, MOBA_W))
    c = GLA_CHUNK if s % GLA_CHUNK == 0 else s
    o_gla, st_p = _gla(qg, kg, vg, la, jnp.zeros((b, GLA_HEADS, GLA_DK, GLA_DV), F32), b, s, c,
                       _row_tile(s), c)
    mk_p, mv_p = _mem_kv(mem_prompt.reshape(b * mem_len, D_MODEL), g_mem_kv[l],
                         w_mem_k[l].astype(BF16), w_mem_v[l].astype(BF16))
    y_p = _layer_tail(xp, o_moba.reshape(b * s, MOBA_W), o_gla, rg, b, s,
                      mk_p.reshape(b, mem_len, D_MODEL), mv_p.reshape(b, mem_len, D_MODEL), wts, None, g_final)

    tp = SAMPLE_TPAD
    xs = jnp.pad(x_sample, ((0, 0), (0, tp - t), (0, 0))).reshape(db * tp, D_MODEL)
    pos_s = past + (jnp.arange(db * tp) % tp)
    qs, ks, vs, qgs, kgs, vgs, rgs, las = _mixer_inputs(xs, pos_s, g_mix[l], w_in_p, wa_p, b_alpha[l], F32)
    ck3 = cache_k[l].reshape(cache_k.shape[1], page, MOBA_W)
    cv3 = cache_v[l].reshape(cache_v.shape[1], page, MOBA_W)
    sel = _sample_gate(page_table, _page_sums(ck3), qs, nfull, ppb)
    sel = sel.reshape(db, tp, LANES)[:, :t, :MOBA_HEADS * MOBA_TOPK].reshape(db, t * MOBA_HEADS * MOBA_TOPK)
    o_moba_s = _sample_attn(page_table, sel, qs, ks, vs, ck3, cv3, t, ppb)
    o_gla_s, st_s = _gla(qgs, kgs, vgs, las, state_gla[l], db, tp, tp, tp, t)
    mem_s = cache_mem_k.shape[2]
    y_s = _layer_tail(xs, o_moba_s, o_gla_s, rgs, db, tp,
                      cache_mem_k[l].reshape(db, mem_s, D_MODEL), cache_mem_v[l].reshape(db, mem_s, D_MODEL),
                      wts, t, g_final)

    heads = (MOBA_HEADS, MOBA_DH)
    k_s = ks.reshape(db, tp, *heads)[:, :t]
    v_s = vs.reshape(db, tp, *heads)[:, :t]
    return (y_p.reshape(b, s, D_MODEL), y_s.reshape(db, t, D_MODEL),
            k.reshape(1, b, s, *heads), v.reshape(1, b, s, *heads), st_p[None],
            mk_p.reshape(1, b, mem_len, MEM_HEADS, MEM_DH), mv_p.reshape(1, b, mem_len, MEM_HEADS, MEM_DH),
            k_s[None], v_s[None], st_s[None])
```

```python
import functools
import math

import numpy as np
import jax
import jax.numpy as jnp
from jax import lax
from jax.experimental import pallas as pl
from jax.experimental.pallas import tpu as pltpu

F32 = jnp.float32
BF16 = jnp.bfloat16
I32 = jnp.int32

EPS = 1e-6
NEG = -1e30
LANES = 128
SUBLANES = 8
VMEM_LIMIT = 56 * 1024 * 1024

D_MODEL = 1024
MOBA_HEADS, MOBA_DH, MOBA_BLOCK, MOBA_TOPK = 8, 64, 256, 3
MOBA_W = MOBA_HEADS * MOBA_DH
ROPE_THETA, ROPE_DIM = 500000.0, 16
GLA_HEADS, GLA_DK, GLA_DV = 4, 64, 128
GLA_KW, GLA_W = GLA_HEADS * GLA_DK, GLA_HEADS * GLA_DV
GLA_LOWRANK, GLA_TAU, GLA_CHUNK = 16, 16.0, 64
IN_COLS = 3 * MOBA_W + 2 * GLA_KW + 2 * GLA_W + GLA_LOWRANK
IN_COLS_PAD = 3 * MOBA_W + 2 * GLA_KW + 2 * GLA_W + LANES
MEM_HEADS, MEM_DH = 4, 256
PEER_HEADS, PEER_NKEYS, PEER_DQ, PEER_TOPK = 8, 128, 256, 16
PEER_PAIRS = PEER_HEADS * PEER_TOPK
SAMPLE_TPAD = 8
MOBA_GROUP = 4


def _params(sem):
    return pltpu.CompilerParams(dimension_semantics=sem, vmem_limit_bytes=VMEM_LIMIT)


def _dot(a, b):
    return jnp.dot(a, b, preferred_element_type=F32)


def _dot_nt(a, b):
    return lax.dot_general(a, b, (((1,), (1,)), ((), ())), preferred_element_type=F32)


def _dot_tn(a, b):
    return lax.dot_general(a, b, (((0,), (0,)), ((), ())), preferred_element_type=F32)


def _split3(x):
    x1 = x.astype(BF16)
    r1 = x - x1.astype(F32)
    x2 = r1.astype(BF16)
    x3 = (r1 - x2.astype(F32)).astype(BF16)
    return x1, x2, x3


def _dot_exact_lhs(sel, x):
    a, b, c = _split3(x)
    return _dot(sel, a) + _dot(sel, b) + _dot(sel, c)


def _rms(x, g):
    return x * lax.rsqrt(jnp.mean(x * x, axis=-1, keepdims=True) + EPS) * g


def _row_tile(n):
    return 256 if n % 256 == 0 else n


def _proj_kernel(x_ref, g_ref, w_ref, wa_ref, ba_ref, cos_ref, sa_ref, sb_ref,
                 q_ref, k_ref, v_ref, qg_ref, kg_ref, vg_ref, rg_ref, la_ref):
    h = _rms(x_ref[...], g_ref[...]).astype(BF16)
    cos, sa, sb = cos_ref[...], sa_ref[...], sb_ref[...]

    def rope(z):
        outs = []
        for c in range(MOBA_W // LANES):
            zc = z[:, c * LANES:(c + 1) * LANES]
            outs.append(zc * cos + pltpu.roll(zc, ROPE_DIM // 2, 1) * sa
                        + pltpu.roll(zc, LANES - ROPE_DIM // 2, 1) * sb)
        return jnp.concatenate(outs, axis=1)

    o = 0
    q = rope(_dot(h, w_ref[:, o:o + MOBA_W])); o += MOBA_W
    q_ref[...] = (q * (MOBA_DH ** -0.5)).astype(q_ref.dtype)
    k_ref[...] = rope(_dot(h, w_ref[:, o:o + MOBA_W])); o += MOBA_W
    v_ref[...] = _dot(h, w_ref[:, o:o + MOBA_W]); o += MOBA_W
    qg_ref[...] = _dot(h, w_ref[:, o:o + GLA_KW]) * (GLA_DK ** -0.5); o += GLA_KW
    kg_ref[...] = _dot(h, w_ref[:, o:o + GLA_KW]); o += GLA_KW
    vg_ref[...] = _dot(h, w_ref[:, o:o + GLA_W]); o += GLA_W
    rg_ref[...] = _dot(h, w_ref[:, o:o + GLA_W]); o += GLA_W
    ag = _dot(h, w_ref[:, o:o + LANES])
    xg = _dot(ag.astype(BF16), wa_ref[...]) + ba_ref[...]
    log_sig = jnp.minimum(xg, 0.0) - jnp.log(1.0 + jnp.exp(-jnp.abs(xg)))
    la_ref[...] = log_sig * (1.0 / GLA_TAU)


def _rope_tables(pos):
    half = ROPE_DIM // 2
    inv_freq = 1.0 / (ROPE_THETA ** (jnp.arange(half, dtype=F32) * (2.0 / ROPE_DIM)))
    ang = pos.astype(F32)[:, None] * inv_freq[None, :]
    cos, sin = jnp.cos(ang), jnp.sin(ang)
    n = pos.shape[0]
    one = jnp.ones((n, MOBA_DH - ROPE_DIM), F32)
    zero = jnp.zeros((n, MOBA_DH - ROPE_DIM), F32)
    zh = jnp.zeros((n, half), F32)
    c64 = jnp.concatenate([cos, cos, one], axis=1)
    sa64 = jnp.concatenate([zh, sin, zero], axis=1)
    sb64 = jnp.concatenate([-sin, zh, zero], axis=1)
    rep = LANES // MOBA_DH
    return (jnp.tile(c64, (1, rep)), jnp.tile(sa64, (1, rep)), jnp.tile(sb64, (1, rep)))


def _mixer_inputs(x2d, pos_rows, g_mix, w_in_p, wa_p, b_alpha, q_dtype):
    n = x2d.shape[0]
    tm = _row_tile(n)
    nrep = pos_rows.shape[0] // tm
    cos, sa, sb = _rope_tables(pos_rows)
    row = lambda w: pl.BlockSpec((tm, w), lambda i: (i, 0))
    full = lambda a: pl.BlockSpec(a.shape, lambda i: (0, 0))
    tab = pl.BlockSpec((tm, LANES), lambda i: (i % nrep, 0))
    g2 = g_mix.reshape(1, D_MODEL)
    b2 = b_alpha.reshape(1, GLA_KW)
    widths = (MOBA_W, MOBA_W, MOBA_W, GLA_KW, GLA_KW, GLA_W, GLA_W, GLA_KW)
    dts = (q_dtype, F32, F32, F32, F32, F32, F32, F32)
    return pl.pallas_call(
        _proj_kernel,
        grid=(n // tm,),
        in_specs=[row(D_MODEL), full(g2), full(w_in_p), full(wa_p), full(b2), tab, tab, tab],
        out_specs=[row(w) for w in widths],
        out_shape=[jax.ShapeDtypeStruct((n, w), dt) for w, dt in zip(widths, dts)],
        compiler_params=_params(("arbitrary",)),
        name="mixer_inputs",
    )(x2d, g2, w_in_p, wa_p, b2, cos, sa, sb)


def _moba_prompt_kernel(q_ref, k_ref, v_ref, o_ref, km_ref, m_ref, l_ref, acc_ref, *, nblk):
    i = pl.program_id(2)
    blk = MOBA_BLOCK
    hpl = LANES // MOBA_DH

    @pl.when(i == 0)
    def _():
        for j in range(nblk):
            km_ref[j:j + 1, :] = jnp.mean(k_ref[0, j * blk:(j + 1) * blk, :], axis=0, keepdims=True)

    q = q_ref[0]
    jrow = lax.broadcasted_iota(I32, (nblk, blk), 0)
    rowi = lax.broadcasted_iota(I32, (blk, blk), 0)
    coli = lax.broadcasted_iota(I32, (blk, blk), 1)
    lane_q = lax.broadcasted_iota(I32, (blk, LANES), 1)
    lane_m = lax.broadcasted_iota(I32, (nblk, LANES), 1)
    blk_m = lax.broadcasted_iota(I32, (nblk, LANES), 0)
    off = pl.multiple_of(i * blk, blk)
    km = km_ref[...]
    q_aug, q_own, own, onehot = [], [], [], []
    for hh in range(hpl):
        own_q = (lane_q // MOBA_DH) == hh
        own.append(own_q)
        q_own.append(jnp.where(own_q, q, jnp.zeros_like(q)))
        k1, k2, k3 = _split3(jnp.where((lane_m // MOBA_DH) == hh, km, 0.0))
        gt = _dot_nt(k1, q) + _dot_nt(k2, q) + _dot_nt(k3, q)
        gm = jnp.where(jrow < i, gt, NEG)
        cnt = jnp.zeros((nblk, blk), I32)
        for jp in range(nblk):
            gj = gm[jp:jp + 1, :]
            beats = (gj > gm) | ((gj == gm) & (jp < jrow))
            cnt = cnt + beats.astype(I32)
        picked = (cnt < MOBA_TOPK) & (jrow < i)
        pen_t = jnp.where(picked, 0.0, NEG).astype(BF16)
        place = (lane_m == blk_m + (hpl - 1 - hh) * MOBA_DH).astype(BF16)
        onehot.append(place)
        pen = _dot_tn(pen_t, place).astype(BF16)
        q_aug.append(jnp.where(own_q, q, pen))

    def widen(x):
        return jnp.concatenate([x, x], axis=1)

    def attend(tiles, first):
        for hh in range(hpl):
            scores = []
            for kt, _, j in tiles:
                if j is None:
                    sj = _dot_nt(q_own[hh], kt)
                    sj = jnp.where(coli <= rowi, sj, NEG)
                else:
                    mark = jnp.broadcast_to(onehot[hh][j:j + 1, :], (blk, LANES))
                    sj = _dot_nt(q_aug[hh], jnp.where(own[hh], kt, mark))
                scores.append(sj)
            top = scores[0]
            for sj in scores[1:]:
                top = jnp.maximum(top, sj)
            m_grp = jnp.max(top, axis=1, keepdims=True)
            if first:
                m_new = jnp.broadcast_to(m_grp, (blk, LANES))
            else:
                m_old = m_ref[hh]
                m_new = jnp.maximum(m_old, m_grp)
                alpha = jnp.exp(m_old - m_new)
            m_wide = widen(m_new)
            psum, acc = None, None
            for sj, (_, vt, _) in zip(scores, tiles):
                pj = jnp.exp(sj - m_wide)
                pv = _dot(pj.astype(BF16), vt)
                psum = pj if psum is None else psum + pj
                acc = pv if acc is None else acc + pv
            l_grp = jnp.sum(psum, axis=1, keepdims=True)
            if first:
                l_ref[hh] = jnp.broadcast_to(l_grp, (blk, LANES))
                acc_ref[hh] = acc
            else:
                l_ref[hh] = alpha * l_ref[hh] + l_grp
                acc_ref[hh] = alpha * acc_ref[hh] + acc
            m_ref[hh] = m_new

    attend([(k_ref[0, pl.ds(off, blk), :].astype(BF16), v_ref[0, pl.ds(off, blk), :].astype(BF16), None)], True)

    for j0 in range(0, nblk - 1, MOBA_GROUP):
        @pl.when(i > j0)
        def _(j0=j0):
            attend([(k_ref[0, j * blk:(j + 1) * blk, :].astype(BF16),
                     v_ref[0, j * blk:(j + 1) * blk, :].astype(BF16), j)
                    for j in range(j0, min(j0 + MOBA_GROUP, nblk - 1))], False)

    o_ref[0] = jnp.where(own[0], acc_ref[0] / l_ref[0], acc_ref[1] / l_ref[1])


def _moba_prompt(q, k, v):
    b, s, w = q.shape
    assert s % MOBA_BLOCK == 0 and w == MOBA_W
    nblk = s // MOBA_BLOCK
    assert nblk % SUBLANES == 0 or nblk < SUBLANES
    hpl = LANES // MOBA_DH
    assert hpl == 2
    kv_spec = pl.BlockSpec((1, s, LANES), lambda bi, hp, i: (bi, 0, hp))
    q_spec = pl.BlockSpec((1, MOBA_BLOCK, LANES), lambda bi, hp, i: (bi, i, hp))
    return pl.pallas_call(
        functools.partial(_moba_prompt_kernel, nblk=nblk),
        grid=(b, w // LANES, nblk),
        in_specs=[q_spec, kv_spec, kv_spec],
        out_specs=q_spec,
        out_shape=jax.ShapeDtypeStruct((b, s, w), F32),
        scratch_shapes=[pltpu.VMEM((nblk, LANES), F32)]
        + [pltpu.VMEM((hpl, MOBA_BLOCK, LANES), F32)] * 3,
        compiler_params=_params(("arbitrary", "arbitrary", "arbitrary")),
        name="moba_prompt",
    )(q, k, v)


def _gla_consts(c):
    nlev = int(math.log2(c))
    assert 2 ** nlev == c
    r = np.arange(c)
    tri = (r[:, None] >= r[None, :]).astype(np.float32)
    sel = np.zeros((nlev, c, c), np.float32)
    msk = np.zeros((nlev + 1, c, c), np.float32)
    for l in range(nlev):
        b = 2 ** l
        ref_row = (r // (2 * b)) * (2 * b) + b - 1
        sel[l, r, ref_row] = 1.0
        same = (r[:, None] // (2 * b)) == (r[None, :] // (2 * b))
        msk[l] = same & ((r[:, None] % (2 * b)) >= b) & ((r[None, :] % (2 * b)) < b)
    msk[nlev] = np.eye(c)
    return jnp.asarray(tri, BF16), jnp.asarray(sel, BF16), jnp.asarray(msk, F32)


def _gla_kernel(q_ref, k_ref, v_ref, la_ref, s0_ref, tri_ref, sel_ref, msk_ref, o_ref, so_ref, st_ref,
                *, c, nchunk, nlev, t_real, nsteps):
    step = pl.program_id(1)

    @pl.when(step == 0)
    def _():
        st_ref[...] = s0_ref[0]

    eye = (lax.broadcasted_iota(I32, (GLA_DK, GLA_DK), 0)
           == lax.broadcasted_iota(I32, (GLA_DK, GLA_DK), 1)).astype(F32)
    for ci in range(nchunk):
        rows = slice(ci * c, (ci + 1) * c)
        q, k, v, la = q_ref[rows, :], k_ref[rows, :], v_ref[rows, :], la_ref[rows, :]
        if t_real < c:
            live = lax.broadcasted_iota(I32, (c, 1), 0) < t_real
            la = jnp.where(live, la, 0.0)
            k = jnp.where(live, k, 0.0)
        a = _dot_exact_lhs(tri_ref[...], la)
        a_last = a[c - 1:c, :]
        qe = (q * jnp.exp(a)).astype(BF16)
        kd = (k * jnp.exp(a_last - a)).astype(BF16)
        e_last = jnp.exp(a_last)
        qb, kb, vb = q.astype(BF16), k.astype(BF16), v.astype(BF16)
        qt, kt = [], []
        for l in range(nlev):
            aref = _dot_exact_lhs(sel_ref[l], a)
            qt.append((q * jnp.exp(jnp.minimum(a - aref, 0.0))).astype(BF16))
            kt.append((k * jnp.exp(jnp.minimum(aref - a, 0.0))).astype(BF16))
        outs = []
        for h in range(GLA_HEADS):
            hs = slice(h * GLA_DK, (h + 1) * GLA_DK)
            vs = slice(h * GLA_DV, (h + 1) * GLA_DV)
            att = msk_ref[nlev] * _dot_nt(qb[:, hs], kb[:, hs])
            for l in range(nlev):
                att = att + msk_ref[l] * _dot_nt(qt[l][:, hs], kt[l][:, hs])
            sh = st_ref[h]
            outs.append(_dot(att.astype(BF16), vb[:, vs]) + _dot(qe[:, hs], sh.astype(BF16)))
            e_col = jnp.sum(eye * e_last[:, hs], axis=1, keepdims=True)
            st_ref[h] = e_col * sh + _dot_tn(kd[:, hs], vb[:, vs])
        o_ref[rows, :] = jnp.concatenate(outs, axis=1)

    @pl.when(step == nsteps - 1)
    def _():
        so_ref[0] = st_ref[...]


def _gla(qg, kg, vg, la, s0, b, t, c, rows_per_step, t_real):
    assert t % rows_per_step == 0 and rows_per_step % c == 0
    nsteps = t // rows_per_step
    nlev = int(math.log2(c))
    tri, sel, msk = _gla_consts(c)
    row = lambda w: pl.BlockSpec((rows_per_step, w), lambda bi, i: (bi * nsteps + i, 0))
    st = pl.BlockSpec((1, GLA_HEADS, GLA_DK, GLA_DV), lambda bi, i: (bi, 0, 0, 0))
    const2 = pl.BlockSpec(tri.shape, lambda bi, i: (0, 0))
    const3 = lambda a: pl.BlockSpec(a.shape, lambda bi, i: (0, 0, 0))
    return pl.pallas_call(
        functools.partial(_gla_kernel, c=c, nchunk=rows_per_step // c, nlev=nlev, t_real=t_real,
                          nsteps=nsteps),
        grid=(b, nsteps),
        in_specs=[row(GLA_KW), row(GLA_KW), row(GLA_W), row(GLA_KW), st, const2, const3(sel), const3(msk)],
        out_specs=[row(GLA_W), st],
        out_shape=[jax.ShapeDtypeStruct((b * t, GLA_W), F32),
                   jax.ShapeDtypeStruct((b, GLA_HEADS, GLA_DK, GLA_DV), F32)],
        scratch_shapes=[pltpu.VMEM((GLA_HEADS, GLA_DK, GLA_DV), F32)],
        compiler_params=_params(("arbitrary", "arbitrary")),
        name="gla",
    )(qg, kg, vg, la, s0, tri, sel, msk)


def _mixer_out_kernel(x_ref, om_ref, og_ref, rg_ref, gg_ref, wo_ref, gq_ref, wq_ref, x1_ref, qm_ref):
    og = og_ref[...]
    parts = []
    for h in range(GLA_HEADS):
        oh = og[:, h * GLA_DV:(h + 1) * GLA_DV]
        parts.append(_rms(oh, gg_ref[...]))
    rg = rg_ref[...]
    ogn = jnp.concatenate(parts, axis=1) * (rg * jax.nn.sigmoid(rg))
    cat = jnp.concatenate([om_ref[...], ogn], axis=1).astype(BF16)
    x1 = x_ref[...] + _dot(cat, wo_ref[...])
    x1_ref[...] = x1
    qm_ref[...] = _dot(_rms(x1, gq_ref[...]).astype(BF16), wq_ref[...]).astype(qm_ref.dtype)


def _mixer_out(x2d, o_moba, o_gla, rg, g_gla_out, w_out, g_mem_q, w_mem_q):
    n = x2d.shape[0]
    tm = _row_tile(n)
    row = lambda w: pl.BlockSpec((tm, w), lambda i: (i, 0))
    full = lambda a: pl.BlockSpec(a.shape, lambda i: (0, 0))
    gg = g_gla_out.reshape(1, GLA_DV)
    gq = g_mem_q.reshape(1, D_MODEL)
    return pl.pallas_call(
        _mixer_out_kernel,
        grid=(n // tm,),
        in_specs=[row(D_MODEL), row(MOBA_W), row(GLA_W), row(GLA_W), full(gg), full(w_out), full(gq),
                  full(w_mem_q)],
        out_specs=[row(D_MODEL), row(D_MODEL)],
        out_shape=[jax.ShapeDtypeStruct((n, D_MODEL), F32), jax.ShapeDtypeStruct((n, D_MODEL), BF16)],
        compiler_params=_params(("arbitrary",)),
        name="mixer_out",
    )(x2d, o_moba, o_gla, rg, gg, w_out, gq, w_mem_q)


def _mem_kv_kernel(m_ref, g_ref, wk_ref, wv_ref, mk_ref, mv_ref):
    h = _rms(m_ref[...], g_ref[...]).astype(BF16)
    mk_ref[...] = _dot(h, wk_ref[...])
    mv_ref[...] = _dot(h, wv_ref[...])


def _mem_kv(mem2d, g_mem_kv, w_mem_k, w_mem_v):
    n = mem2d.shape[0]
    tm = _row_tile(n)
    row = pl.BlockSpec((tm, D_MODEL), lambda i: (i, 0))
    full = lambda a: pl.BlockSpec(a.shape, lambda i: (0, 0))
    g = g_mem_kv.reshape(1, D_MODEL)
    return pl.pallas_call(
        _mem_kv_kernel,
        grid=(n // tm,),
        in_specs=[row, full(g), full(w_mem_k), full(w_mem_v)],
        out_specs=[row, row],
        out_shape=[jax.ShapeDtypeStruct((n, D_MODEL), F32)] * 2,
        compiler_params=_params(("arbitrary",)),
        name="memory_kv",
    )(mem2d, g, w_mem_k, w_mem_v)


def _mem_attend_kernel(q_ref, mk_ref, mv_ref, x1_ref, wo_ref, gf_ref, x2_ref, h3_ref):
    q = q_ref[...]
    mk = mk_ref[0].astype(BF16)
    mv = mv_ref[0].astype(BF16)
    outs = []
    for h in range(MEM_HEADS):
        hs = slice(h * MEM_DH, (h + 1) * MEM_DH)
        s = _dot_nt(q[:, hs], mk[:, hs]) * (MEM_DH ** -0.5)
        p = jnp.exp(s - jnp.max(s, axis=1, keepdims=True))
        l = jnp.sum(p, axis=1, keepdims=True)
        outs.append(_dot(p.astype(BF16), mv[:, hs]) / l)
    o = jnp.concatenate(outs, axis=1).astype(BF16)
    x2 = x1_ref[...] + _dot(o, wo_ref[...])
    x2_ref[...] = x2
    h3_ref[...] = _rms(x2, gf_ref[...])


def _mem_attend(qmem, mk, mv, x1, w_mem_o, g_ffn, b, t):
    tq = _row_tile(t)
    nq = t // tq
    row = pl.BlockSpec((tq, D_MODEL), lambda bi, i: (bi * nq + i, 0))
    mem = pl.BlockSpec((1,) + mk.shape[1:], lambda bi, i: (bi, 0, 0))
    full = lambda a: pl.BlockSpec(a.shape, lambda bi, i: (0, 0))
    gf = g_ffn.reshape(1, D_MODEL)
    return pl.pallas_call(
        _mem_attend_kernel,
        grid=(b, nq),
        in_specs=[row, mem, mem, row, full(w_mem_o), full(gf)],
        out_specs=[row, row],
        out_shape=[jax.ShapeDtypeStruct((b * t, D_MODEL), F32)] * 2,
        compiler_params=_params(("arbitrary", "arbitrary")),
        name="memory_attend",
    )(qmem, mk, mv, x1, w_mem_o, gf)


def _peer_candidates():
    k = PEER_TOPK
    flat = []
    flat += [0 * k + j for j in range(k)]
    for i in range(1, 8):
        flat += [i * k + j if (i + 1) * (j + 1) <= k else -1 for j in range(8)]
    flat += [(8 + r) * k for r in range(8)]
    return np.asarray(flat, np.int32)


def _route_kernel(h_ref, wq_ref, keys_ref, flat_ref, eid_ref, g_ref):
    tq = h_ref.shape[0]
    k = PEER_TOPK
    hb = h_ref[...].astype(BF16)
    kio = lax.broadcasted_iota(I32, (PEER_NKEYS, tq), 0)
    r16 = lax.broadcasted_iota(I32, (k, tq), 0)
    flat = flat_ref[...]
    nexp = PEER_NKEYS * PEER_NKEYS
    big = jnp.int32(2 ** 30)
    qv_all = _dot(hb, wq_ref[...]).astype(BF16)

    def top16(s):
        def body(kk, carry):
            s, vals, idxs = carry
            m = jnp.max(s, axis=0, keepdims=True)
            idx = jnp.min(jnp.where(s == m, kio, PEER_NKEYS), axis=0, keepdims=True)
            vals = jnp.where(r16 == kk, m, vals)
            idxs = jnp.where(r16 == kk, idx, idxs)
            s = jnp.where(kio == idx, -jnp.inf, s)
            return s, vals, idxs
        _, vals, idxs = lax.fori_loop(0, k, body, (s, jnp.zeros((k, tq), F32), jnp.zeros((k, tq), I32)))
        return vals, idxs

    g_all, e_all = [], []
    for h in range(PEER_HEADS):
        tv, ti = [], []
        for p in range(2):
            hp = h * 2 + p
            qv = qv_all[:, hp * (PEER_DQ // 2):(hp + 1) * (PEER_DQ // 2)]
            s = _dot_nt(keys_ref[hp], qv)
            vals, idxs = top16(s)
            tv.append(vals)
            ti.append(idxs)
        a, b = tv
        ia, ib = ti
        cs = [a[0:1] + b]
        ce = [ia[0:1] * PEER_NKEYS + ib]
        for i in range(1, 8):
            cs.append(a[i:i + 1] + b[0:8])
            ce.append(ia[i:i + 1] * PEER_NKEYS + ib[0:8])
        cs.append(a[8:16] + b[0:1])
        ce.append(ia[8:16] * PEER_NKEYS + ib[0:1])
        cand = jnp.where(flat >= 0, jnp.concatenate(cs, axis=0), -jnp.inf)
        ckey = jnp.where(flat >= 0, flat * nexp + jnp.concatenate(ce, axis=0), big)

        def body2(kk, carry):
            cand, vals, eids = carry
            m = jnp.max(cand, axis=0, keepdims=True)
            pick = jnp.min(jnp.where(cand == m, ckey, big), axis=0, keepdims=True)
            vals = jnp.where(r16 == kk, m, vals)
            eids = jnp.where(r16 == kk, pick & (nexp - 1), eids)
            cand = jnp.where(ckey == pick, -jnp.inf, cand)
            return cand, vals, eids
        _, fin, eids = lax.fori_loop(0, k, body2, (cand, jnp.zeros((k, tq), F32), jnp.zeros((k, tq), I32)))
        ex = jnp.exp(fin - fin[0:1])
        g_all.append(ex / jnp.sum(ex, axis=0, keepdims=True))
        e_all.append(lax.bitcast_convert_type(eids, F32))
    g_ref[...] = jnp.concatenate(g_all, axis=0).T
    eid_ref[...] = lax.bitcast_convert_type(jnp.concatenate(e_all, axis=0).T, I32)


def _route(h3, wq_hp, keys_hp):
    n = h3.shape[0]
    tq = _row_tile(n)
    flat = jnp.broadcast_to(jnp.asarray(_peer_candidates())[:, None], (_peer_candidates().shape[0], tq))
    out = pl.BlockSpec((tq, PEER_PAIRS), lambda i: (i, 0))
    return pl.pallas_call(
        _route_kernel,
        grid=(n // tq,),
        in_specs=[pl.BlockSpec((tq, D_MODEL), lambda i: (i, 0)),
                  pl.BlockSpec(wq_hp.shape, lambda i: (0, 0)),
                  pl.BlockSpec(keys_hp.shape, lambda i: (0, 0, 0)),
                  pl.BlockSpec(flat.shape, lambda i: (0, 0))],
        out_specs=[out, out],
        out_shape=[jax.ShapeDtypeStruct((n, PEER_PAIRS), I32), jax.ShapeDtypeStruct((n, PEER_PAIRS), F32)],
        compiler_params=_params(("arbitrary",)),
        name="peer_route",
    )(h3, wq_hp, keys_hp, flat)


PEER_TT = 8
PEER_NBUF = 4
PEER_AHEAD = 2
ROW_WORDS = D_MODEL // 2
ROW_SUB = ROW_WORDS // LANES
REC_SUB = 2 * ROW_SUB
assert REC_SUB == SUBLANES


def _pack_experts(u, v):
    def words(t):
        tb = lax.bitcast_convert_type(t.astype(BF16), jnp.uint16).astype(jnp.uint32)
        w = tb[:, :ROW_WORDS] | (tb[:, ROW_WORDS:] << 16)
        return w.reshape(t.shape[0], ROW_SUB, LANES)
    tab = jnp.concatenate([words(u), words(v)], axis=1)
    return lax.bitcast_convert_type(tab, I32).reshape(u.shape[0] * REC_SUB, LANES)


def _unpack(w):
    lo = lax.bitcast_convert_type(w << 16, F32)
    hi = lax.bitcast_convert_type(w & jnp.int32(-65536), F32)
    return lo, hi


def _peer_kernel(eid_ref, eidn_ref, g_ref, h_ref, x2_ref, gf_ref, tab_ref, y_ref, *rest, nsteps):
    bufs, sem, po_ref = rest[:PEER_NBUF], rest[PEER_NBUF], rest[PEER_NBUF + 1]
    i = pl.program_id(0)
    tile_rows = PEER_TT * PEER_PAIRS * REC_SUB

    def gather(eref, row0, k, t):
        for p in range(PEER_PAIRS):
            src = pl.multiple_of(eref[row0 + t, p] * REC_SUB, REC_SUB)
            dst = pl.multiple_of((t * PEER_PAIRS + p) * REC_SUB, REC_SUB)
            pltpu.make_async_copy(tab_ref.at[pl.ds(src, REC_SUB), :],
                                  bufs[k].at[pl.ds(dst, REC_SUB), :], sem.at[k]).start(priority=p % 2)

    def wait_tile(k):
        pltpu.make_async_copy(tab_ref.at[pl.ds(0, tile_rows), :], bufs[k], sem.at[k]).wait()

    eye = (lax.broadcasted_iota(I32, (PEER_PAIRS, PEER_PAIRS), 0)
           == lax.broadcasted_iota(I32, (PEER_PAIRS, PEER_PAIRS), 1)).astype(F32)

    def consume(k, t):
        buf = bufs[k]
        row = k * PEER_TT + t
        base = t * (PEER_PAIRS * REC_SUB)
        x = h_ref[pl.ds(row, 1), :]
        dacc = jnp.zeros((PEER_PAIRS, LANES), F32)
        for s in range(ROW_SUB):
            lo, hi = _unpack(buf[pl.ds(base + s, PEER_PAIRS, stride=REC_SUB), :])
            dacc = dacc + lo * x[:, s * LANES:(s + 1) * LANES] \
                + hi * x[:, ROW_WORDS + s * LANES:ROW_WORDS + (s + 1) * LANES]
        d = jnp.sum(dacc, axis=1, keepdims=True)
        act = 0.5 * d * (1.0 + lax.erf(d * (2.0 ** -0.5)))
        gcol = jnp.sum(eye * g_ref[pl.ds(row, 1), :], axis=1, keepdims=True)
        wcol = gcol * act
        los, his = [], []
        for s in range(ROW_SUB, REC_SUB):
            lo, hi = _unpack(buf[pl.ds(base + s, PEER_PAIRS, stride=REC_SUB), :])
            los.append(jnp.sum(wcol * lo, axis=0, keepdims=True))
            his.append(jnp.sum(wcol * hi, axis=0, keepdims=True))
        po_ref[pl.ds(row, 1), :] = jnp.concatenate(los + his, axis=1)

    @pl.when(i == 0)
    def _():
        for k in range(PEER_AHEAD):
            def first(t, c, k=k):
                gather(eid_ref, k * PEER_TT, k, t)
                return c
            lax.fori_loop(0, PEER_TT, first, 0)

    def phase(t, c, k, prefetch):
        if prefetch:
            ahead = k + PEER_AHEAD
            if ahead < PEER_NBUF:
                gather(eid_ref, ahead * PEER_TT, ahead, t)
            else:
                gather(eidn_ref, (ahead - PEER_NBUF) * PEER_TT, ahead - PEER_NBUF, t)
        consume(k, t)
        return c

    n_next = jnp.where(i + 1 < nsteps, PEER_TT, 0)
    for k in range(PEER_NBUF):
        wait_tile(k)
        n_pre = PEER_TT if k + PEER_AHEAD < PEER_NBUF else n_next
        lax.fori_loop(0, n_pre, functools.partial(phase, k=k, prefetch=True), 0)
        if k + PEER_AHEAD >= PEER_NBUF:
            lax.fori_loop(n_pre, PEER_TT, functools.partial(phase, k=k, prefetch=False), 0)
    y_ref[...] = _rms(x2_ref[...] + po_ref[...], gf_ref[...])


def _peer_mix(eid, g, h3, x2, g_final, table):
    n = h3.shape[0]
    rows = PEER_NBUF * PEER_TT
    assert n % rows == 0
    nsteps = n // rows
    row = lambda w: pl.BlockSpec((rows, w), lambda i: (i, 0))
    gf = g_final.reshape(1, D_MODEL)
    buf = pltpu.VMEM((PEER_TT * PEER_PAIRS * REC_SUB, LANES), I32)
    return pl.pallas_call(
        functools.partial(_peer_kernel, nsteps=nsteps),
        grid=(nsteps,),
        in_specs=[pl.BlockSpec((rows, PEER_PAIRS), lambda i: (i, 0), memory_space=pltpu.SMEM),
                  pl.BlockSpec((rows, PEER_PAIRS), lambda i: (jnp.minimum(i + 1, nsteps - 1), 0),
                               memory_space=pltpu.SMEM),
                  row(PEER_PAIRS), row(D_MODEL), row(D_MODEL),
                  pl.BlockSpec((1, D_MODEL), lambda i: (0, 0)),
                  pl.BlockSpec(memory_space=pl.ANY)],
        out_specs=row(D_MODEL),
        out_shape=jax.ShapeDtypeStruct((n, D_MODEL), F32),
        scratch_shapes=[buf] * PEER_NBUF + [pltpu.SemaphoreType.DMA((PEER_NBUF,)),
                                            pltpu.VMEM((rows, D_MODEL), F32)],
        compiler_params=_params(("arbitrary",)),
        name="peer_mix",
    )(eid, eid, g, h3, x2, gf, table)


def _page_sum_kernel(k_ref, o_ref):
    pb = k_ref.shape[0]
    hpl = LANES // MOBA_DH
    d_i = lax.broadcasted_iota(I32, (MOBA_DH, LANES), 0)
    l_i = lax.broadcasted_iota(I32, (MOBA_DH, LANES), 1)
    place = [(l_i == d_i + hh * MOBA_DH).astype(F32) for hh in range(hpl)]

    def one_page(pg, c):
        rows = []
        for hp in range(MOBA_HEADS // hpl):
            row = jnp.zeros((1, LANES), F32)
            for hh in range(hpl):
                r = jnp.sum(k_ref[pg, hp * hpl + hh], axis=1, keepdims=True)
                row = row + jnp.sum(r * place[hh], axis=0, keepdims=True)
            rows.append(row)
        o_ref[pl.ds(pg, 1), :] = jnp.concatenate(rows, axis=1)
        return c

    lax.fori_loop(0, pb, one_page, 0)


def _page_sums(cache_kt):
    p, nh, dh, page = cache_kt.shape
    pb = max(d for d in range(1, 17) if p % d == 0 and (d % SUBLANES == 0 or d == p))
    return pl.pallas_call(
        _page_sum_kernel,
        grid=(p // pb,),
        in_specs=[pl.BlockSpec((pb, nh, dh, page), lambda i: (i, 0, 0, 0))],
        out_specs=pl.BlockSpec((pb, nh * dh), lambda i: (i, 0)),
        out_shape=jax.ShapeDtypeStruct((p, nh * dh), F32),
        compiler_params=_params(("arbitrary",)),
        name="page_sums",
    )(cache_kt)


def _sample_gate_kernel(pt_ref, ps_ref, q_ref, sel_ref, km_ref, *, nfull, ppb):
    b = pl.program_id(0)
    for j in range(nfull):
        acc = ps_ref[pl.ds(pt_ref[b, j * ppb], 1), :]
        for pp in range(1, ppb):
            acc = acc + ps_ref[pl.ds(pt_ref[b, j * ppb + pp], 1), :]
        km_ref[j:j + 1, :] = acc * (1.0 / MOBA_BLOCK)
    q = q_ref[...].astype(BF16)
    lane = lax.broadcasted_iota(I32, (SAMPLE_TPAD, nfull), 1)
    olane = lax.broadcasted_iota(I32, (SAMPLE_TPAD, LANES), 1)
    out = jnp.zeros((SAMPLE_TPAD, LANES), I32)
    for h in range(MOBA_HEADS):
        ls = slice(h * MOBA_DH, (h + 1) * MOBA_DH)
        k1, k2, k3 = _split3(km_ref[:, ls])
        qh = q[:, ls]
        g = _dot_nt(qh, k1) + _dot_nt(qh, k2) + _dot_nt(qh, k3)
        for kk in range(MOBA_TOPK):
            m = jnp.max(g, axis=1, keepdims=True)
            idx = jnp.min(jnp.where(g == m, lane, nfull), axis=1, keepdims=True)
            out = out + jnp.where(olane == h * MOBA_TOPK + kk, idx, 0)
            g = jnp.where(lane == idx, -jnp.inf, g)
    sel_ref[...] = out


def _sample_gate(page_table, page_sum, q_s, nfull, ppb):
    db = page_table.shape[0]
    return pl.pallas_call(
        functools.partial(_sample_gate_kernel, nfull=nfull, ppb=ppb),
        grid_spec=pltpu.PrefetchScalarGridSpec(
            num_scalar_prefetch=1,
            grid=(db,),
            in_specs=[pl.BlockSpec(page_sum.shape, lambda b, pt: (0, 0)),
                      pl.BlockSpec((SAMPLE_TPAD, MOBA_W), lambda b, pt: (b, 0))],
            out_specs=pl.BlockSpec((SAMPLE_TPAD, LANES), lambda b, pt: (b, 0)),
            scratch_shapes=[pltpu.VMEM((nfull, MOBA_W), F32)]),
        out_shape=jax.ShapeDtypeStruct((db * SAMPLE_TPAD, LANES), I32),
        compiler_params=_params(("arbitrary",)),
        name="sample_gate",
    )(page_table, page_sum, q_s)


def _sample_attn_kernel(pt_ref, sel_ref, q_ref, kn_ref, vn_ref, ck_ref, cv_ref, o_ref, kbuf, vbuf, sem,
                        *, t_real, ppb, page, nsteps):
    b = pl.program_id(0)
    hp = pl.program_id(1)
    nhp = MOBA_HEADS // (LANES // MOBA_DH)
    hpl = LANES // MOBA_DH
    nslab = t_real * hpl * MOBA_TOPK * ppb
    step = b * nhp + hp
    slot = step % 2

    def copies(bb, hpp, sl, n):
        t, r = divmod(n, hpl * MOBA_TOPK * ppb)
        hh, r = divmod(r, MOBA_TOPK * ppb)
        kk, pp = divmod(r, ppb)
        h = hpp * hpl + hh
        blk = sel_ref[bb, (t * MOBA_HEADS + h) * MOBA_TOPK + kk]
        phys = pt_ref[bb, blk * ppb + pp]
        return (pltpu.make_async_copy(ck_ref.at[phys, h], kbuf.at[sl, n], sem.at[0, sl]),
                pltpu.make_async_copy(cv_ref.at[phys, h], vbuf.at[sl, n], sem.at[1, sl]))

    def fetch(bb, hpp, sl):
        for n in range(nslab):
            ck, cv = copies(bb, hpp, sl, n)
            ck.start()
            cv.start()

    @pl.when(step == 0)
    def _():
        fetch(b, hp, slot)

    @pl.when(step + 1 < nsteps)
    def _():
        nxt = step + 1
        fetch(nxt // nhp, nxt % nhp, 1 - slot)

    for n in range(nslab):
        ck, cv = copies(b, hp, slot, n)
        ck.wait()
        cv.wait()

    q = q_ref[...].astype(BF16)
    kn = kn_ref[...].astype(BF16)
    vn = vn_ref[...].astype(BF16)
    rows = lax.broadcasted_iota(I32, (SAMPLE_TPAD, SAMPLE_TPAD), 0)
    cols = lax.broadcasted_iota(I32, (SAMPLE_TPAD, SAMPLE_TPAD), 1)
    rowc = lax.broadcasted_iota(I32, (SAMPLE_TPAD, 1), 0)
    nsel = MOBA_TOPK * ppb
    outs = []
    for hh in range(hpl):
        ls = slice(hh * MOBA_DH, (hh + 1) * MOBA_DH)
        qh = q[:, ls]
        s_new = jnp.where((cols <= rows) & (cols < t_real), _dot_nt(qh, kn[:, ls]), NEG)
        m_new = jnp.max(s_new, axis=1, keepdims=True)
        acc = jnp.zeros((SAMPLE_TPAD, MOBA_DH), F32)
        for t in range(t_real):
            n0 = (t * hpl + hh) * nsel
            kc = jnp.concatenate([kbuf[slot, n0 + r] for r in range(nsel)], axis=1).astype(BF16)
            vc = jnp.concatenate([vbuf[slot, n0 + r] for r in range(nsel)], axis=1).astype(BF16)
            s = _dot(qh, kc)
            m = jnp.maximum(jnp.max(s, axis=1, keepdims=True), m_new)
            p = jnp.exp(s - m)
            pn = jnp.exp(s_new - m)
            l = jnp.sum(p, axis=1, keepdims=True) + jnp.sum(pn, axis=1, keepdims=True)
            o = (_dot_nt(p.astype(BF16), vc) + _dot(pn.astype(BF16), vn[:, ls])) / l
            acc = acc + jnp.where(rowc == t, o, 0.0)
        outs.append(acc)
    o_ref[...] = jnp.concatenate(outs, axis=1)


def _sample_attn(page_table, sel, q_s, k_new, v_new, cache_kt, cache_vt, t_real, ppb):
    db = page_table.shape[0]
    page = cache_kt.shape[3]
    hpl = LANES // MOBA_DH
    nslab = t_real * hpl * MOBA_TOPK * ppb
    row = pl.BlockSpec((SAMPLE_TPAD, LANES), lambda b, hp, pt, sl: (b, hp))
    anyspec = pl.BlockSpec(memory_space=pl.ANY)
    return pl.pallas_call(
        functools.partial(_sample_attn_kernel, t_real=t_real, ppb=ppb, page=page,
                          nsteps=db * (MOBA_W // LANES)),
        grid_spec=pltpu.PrefetchScalarGridSpec(
            num_scalar_prefetch=2,
            grid=(db, MOBA_W // LANES),
            in_specs=[row, row, row, anyspec, anyspec],
            out_specs=row,
            scratch_shapes=[pltpu.VMEM((2, nslab, MOBA_DH, page), F32),
                            pltpu.VMEM((2, nslab, MOBA_DH, page), F32),
                            pltpu.SemaphoreType.DMA((2, 2))]),
        out_shape=jax.ShapeDtypeStruct((db * SAMPLE_TPAD, MOBA_W), F32),
        compiler_params=_params(("arbitrary", "arbitrary")),
        name="sample_attn",
    )(page_table, sel, q_s, k_new, v_new, cache_kt, cache_vt)


def _layer_tail(x2d, o_moba, o_gla, rg, b, t, mk, mv, wts, live_rows, g_last):
    x1, qmem = _mixer_out(x2d, o_moba, o_gla, rg, wts["g_gla_out"], wts["w_out"], wts["g_mem_q"], wts["w_mem_q"])
    x2, h3 = _mem_attend(qmem, mk, mv, x1, wts["w_mem_o"], wts["g_ffn"], b, t)
    if live_rows is not None:
        x2 = x2.reshape(b, t, D_MODEL)[:, :live_rows].reshape(b * live_rows, D_MODEL)
        h3 = h3.reshape(b, t, D_MODEL)[:, :live_rows].reshape(b * live_rows, D_MODEL)
    eid, g = _route(h3, wts["wq_hp"], wts["keys_hp"])
    return _peer_mix(eid, g, h3, x2, g_last, wts["table"])


def kernel(x_prompt, x_sample, cache_k, cache_v, state_gla, cache_mem_k, cache_mem_v, page_table, mem_prompt,
           g_mix, w_in, w_alpha_up, b_alpha, g_gla_out, w_out, g_mem_q, g_mem_kv, w_mem_q, w_mem_k, w_mem_v,
           w_mem_o, g_ffn, w_peer_q, peer_sub_keys, expert_u, expert_v, g_final):
    depth = w_in.shape[0]
    assert depth == 1, "the final norm is fused into the last layer's PEER kernel"
    b, s, _ = x_prompt.shape
    db, t, _ = x_sample.shape
    n_pages = page_table.shape[1]
    page = cache_k.shape[2]
    past = n_pages * page
    ppb = MOBA_BLOCK // page
    nfull = past // MOBA_BLOCK
    assert past % MOBA_BLOCK == 0 and t <= SAMPLE_TPAD and t <= MOBA_BLOCK and nfull >= MOBA_TOPK
    mem_len = mem_prompt.shape[1]

    l = 0
    wts = dict(
        g_gla_out=g_gla_out[l], w_out=w_out[l].astype(BF16), g_mem_q=g_mem_q[l],
        w_mem_q=w_mem_q[l].astype(BF16), w_mem_o=w_mem_o[l].astype(BF16), g_ffn=g_ffn[l],
        wq_hp=w_peer_q[l].astype(BF16),
        keys_hp=peer_sub_keys[l].astype(BF16).transpose(1, 0, 2, 3).reshape(
            PEER_HEADS * 2, PEER_NKEYS, PEER_DQ // 2),
        table=_pack_experts(expert_u[l], expert_v[l]),
    )
    w_in_p = jnp.pad(w_in[l], ((0, 0), (0, IN_COLS_PAD - IN_COLS))).astype(BF16)
    wa_p = jnp.pad(w_alpha_up[l], ((0, LANES - GLA_LOWRANK), (0, 0))).astype(BF16)

    xp = x_prompt.reshape(b * s, D_MODEL)
    q, k, v, qg, kg, vg, rg, la = _mixer_inputs(xp, jnp.arange(s), g_mix[l], w_in_p, wa_p, b_alpha[l], BF16)
    o_moba = _moba_prompt(q.reshape(b, s, MOBA_W), k.reshape(b, s, MOBA_W), v.reshape(b, s, MOBA_W))
    c = GLA_CHUNK if s % GLA_CHUNK == 0 else s
    o_gla, st_p = _gla(qg, kg, vg, la, jnp.zeros((b, GLA_HEADS, GLA_DK, GLA_DV), F32), b, s, c,
                       _row_tile(s), c)
    mk_p, mv_p = _mem_kv(mem_prompt.reshape(b * mem_len, D_MODEL), g_mem_kv[l],
                         w_mem_k[l].astype(BF16), w_mem_v[l].astype(BF16))
    y_p = _layer_tail(xp, o_moba.reshape(b * s, MOBA_W), o_gla, rg, b, s,
                      mk_p.reshape(b, mem_len, D_MODEL), mv_p.reshape(b, mem_len, D_MODEL), wts, None, g_final)

    tp = SAMPLE_TPAD
    xs = jnp.pad(x_sample, ((0, 0), (0, tp - t), (0, 0))).reshape(db * tp, D_MODEL)
    pos_s = past + (jnp.arange(db * tp) % tp)
    qs, ks, vs, qgs, kgs, vgs, rgs, las = _mixer_inputs(xs, pos_s, g_mix[l], w_in_p, wa_p, b_alpha[l], F32)
    ckt = cache_k[l].transpose(0, 2, 3, 1)
    cvt = cache_v[l].transpose(0, 2, 3, 1)
    sel = _sample_gate(page_table, _page_sums(ckt), qs, nfull, ppb)
    sel = sel.reshape(db, tp, LANES)[:, :t, :MOBA_HEADS * MOBA_TOPK].reshape(db, t * MOBA_HEADS * MOBA_TOPK)
    o_moba_s = _sample_attn(page_table, sel, qs, ks, vs, ckt, cvt, t, ppb)
    o_gla_s, st_s = _gla(qgs, kgs, vgs, las, state_gla[l], db, tp, tp, tp, t)
    mem_s = cache_mem_k.shape[2]
    y_s = _layer_tail(xs, o_moba_s, o_gla_s, rgs, db, tp,
                      cache_mem_k[l].reshape(db, mem_s, D_MODEL), cache_mem_v[l].reshape(db, mem_s, D_MODEL),
                      wts, t, g_final)

    heads = (MOBA_HEADS, MOBA_DH)
    k_s = ks.reshape(db, tp, *heads)[:, :t]
    v_s = vs.reshape(db, tp, *heads)[:, :t]
    return (y_p.reshape(b, s, D_MODEL), y_s.reshape(db, t, D_MODEL),
            k.reshape(1, b, s, *heads), v.reshape(1, b, s, *heads), st_p[None],
            mk_p.reshape(1, b, mem_len, MEM_HEADS, MEM_DH), mv_p.reshape(1, b, mem_len, MEM_HEADS, MEM_DH),
            k_s[None], v_s[None], st_s[None])
```

```python
import functools
import math

import numpy as np
import jax
import jax.numpy as jnp
from jax import lax
from jax.experimental import pallas as pl
from jax.experimental.pallas import tpu as pltpu
from jax.experimental.pallas import tpu_sc as plsc

F32 = jnp.float32
BF16 = jnp.bfloat16
I32 = jnp.int32

EPS = 1e-6
NEG = -1e30
LANES = 128
SUBLANES = 8
VMEM_LIMIT = 56 * 1024 * 1024

D_MODEL = 1024
MOBA_HEADS, MOBA_DH, MOBA_BLOCK, MOBA_TOPK = 8, 64, 256, 3
MOBA_W = MOBA_HEADS * MOBA_DH
ROPE_THETA, ROPE_DIM = 500000.0, 16
GLA_HEADS, GLA_DK, GLA_DV = 4, 64, 128
GLA_KW, GLA_W = GLA_HEADS * GLA_DK, GLA_HEADS * GLA_DV
GLA_LOWRANK, GLA_TAU, GLA_CHUNK = 16, 16.0, 64
IN_COLS = 3 * MOBA_W + 2 * GLA_KW + 2 * GLA_W + GLA_LOWRANK
IN_COLS_PAD = 3 * MOBA_W + 2 * GLA_KW + 2 * GLA_W + LANES
MEM_HEADS, MEM_DH = 4, 256
PEER_HEADS, PEER_NKEYS, PEER_DQ, PEER_TOPK = 8, 128, 256, 16
PEER_PAIRS = PEER_HEADS * PEER_TOPK
SAMPLE_TPAD = 8
MOBA_GROUP = 4


def _params(sem):
    return pltpu.CompilerParams(dimension_semantics=sem, vmem_limit_bytes=VMEM_LIMIT)


def _dot(a, b):
    return jnp.dot(a, b, preferred_element_type=F32)


def _dot_nt(a, b):
    return lax.dot_general(a, b, (((1,), (1,)), ((), ())), preferred_element_type=F32)


def _dot_tn(a, b):
    return lax.dot_general(a, b, (((0,), (0,)), ((), ())), preferred_element_type=F32)


def _split3(x):
    x1 = x.astype(BF16)
    r1 = x - x1.astype(F32)
    x2 = r1.astype(BF16)
    x3 = (r1 - x2.astype(F32)).astype(BF16)
    return x1, x2, x3


def _dot_exact_lhs(sel, x):
    a, b, c = _split3(x)
    return _dot(sel, a) + _dot(sel, b) + _dot(sel, c)


def _rms(x, g):
    return x * lax.rsqrt(jnp.mean(x * x, axis=-1, keepdims=True) + EPS) * g


def _row_tile(n, pref=256):
    return pref if n % pref == 0 else (256 if n % 256 == 0 else n)


def _proj_kernel(x_ref, g_ref, w_ref, wa_ref, ba_ref, cos_ref, sa_ref, sb_ref,
                 q_ref, k_ref, v_ref, qg_ref, kg_ref, vg_ref, rg_ref, la_ref):
    h = _rms(x_ref[...], g_ref[...]).astype(BF16)
    cos, sa, sb = cos_ref[...], sa_ref[...], sb_ref[...]

    def rope(z):
        outs = []
        for c in range(MOBA_W // LANES):
            zc = z[:, c * LANES:(c + 1) * LANES]
            outs.append(zc * cos + pltpu.roll(zc, ROPE_DIM // 2, 1) * sa
                        + pltpu.roll(zc, LANES - ROPE_DIM // 2, 1) * sb)
        return jnp.concatenate(outs, axis=1)

    o = 0
    q = rope(_dot(h, w_ref[:, o:o + MOBA_W])); o += MOBA_W
    q_ref[...] = (q * (MOBA_DH ** -0.5)).astype(q_ref.dtype)
    k_ref[...] = rope(_dot(h, w_ref[:, o:o + MOBA_W])); o += MOBA_W
    v_ref[...] = _dot(h, w_ref[:, o:o + MOBA_W]); o += MOBA_W
    qg_ref[...] = _dot(h, w_ref[:, o:o + GLA_KW]) * (GLA_DK ** -0.5); o += GLA_KW
    kg_ref[...] = _dot(h, w_ref[:, o:o + GLA_KW]); o += GLA_KW
    vg_ref[...] = _dot(h, w_ref[:, o:o + GLA_W]); o += GLA_W
    rg_ref[...] = _dot(h, w_ref[:, o:o + GLA_W]); o += GLA_W
    ag = _dot(h, w_ref[:, o:o + LANES])
    xg = _dot(ag.astype(BF16), wa_ref[...]) + ba_ref[...]
    log_sig = jnp.minimum(xg, 0.0) - jnp.log(1.0 + jnp.exp(-jnp.abs(xg)))
    la_ref[...] = log_sig * (1.0 / GLA_TAU)


def _rope_tables(pos):
    half = ROPE_DIM // 2
    inv_freq = 1.0 / (ROPE_THETA ** (jnp.arange(half, dtype=F32) * (2.0 / ROPE_DIM)))
    ang = pos.astype(F32)[:, None] * inv_freq[None, :]
    cos, sin = jnp.cos(ang), jnp.sin(ang)
    n = pos.shape[0]
    one = jnp.ones((n, MOBA_DH - ROPE_DIM), F32)
    zero = jnp.zeros((n, MOBA_DH - ROPE_DIM), F32)
    zh = jnp.zeros((n, half), F32)
    c64 = jnp.concatenate([cos, cos, one], axis=1)
    sa64 = jnp.concatenate([zh, sin, zero], axis=1)
    sb64 = jnp.concatenate([-sin, zh, zero], axis=1)
    rep = LANES // MOBA_DH
    return (jnp.tile(c64, (1, rep)), jnp.tile(sa64, (1, rep)), jnp.tile(sb64, (1, rep)))


def _mixer_inputs(x2d, pos_rows, g_mix, w_in_p, wa_p, b_alpha, q_dtype):
    n = x2d.shape[0]
    tm = _row_tile(n, 512)
    nrep = pos_rows.shape[0] // tm
    cos, sa, sb = _rope_tables(pos_rows)
    row = lambda w: pl.BlockSpec((tm, w), lambda i: (i, 0))
    full = lambda a: pl.BlockSpec(a.shape, lambda i: (0, 0))
    tab = pl.BlockSpec((tm, LANES), lambda i: (i % nrep, 0))
    g2 = g_mix.reshape(1, D_MODEL)
    b2 = b_alpha.reshape(1, GLA_KW)
    widths = (MOBA_W, MOBA_W, MOBA_W, GLA_KW, GLA_KW, GLA_W, GLA_W, GLA_KW)
    dts = (q_dtype, F32, F32, F32, F32, F32, F32, F32)
    return pl.pallas_call(
        _proj_kernel,
        grid=(n // tm,),
        in_specs=[row(D_MODEL), full(g2), full(w_in_p), full(wa_p), full(b2), tab, tab, tab],
        out_specs=[row(w) for w in widths],
        out_shape=[jax.ShapeDtypeStruct((n, w), dt) for w, dt in zip(widths, dts)],
        compiler_params=_params(("arbitrary",)),
        name="mixer_inputs",
    )(x2d, g2, w_in_p, wa_p, b2, cos, sa, sb)


def _moba_prompt_kernel(q_ref, k_ref, v_ref, o_ref, km_ref, m_ref, l_ref, acc_ref, *, nblk):
    i = pl.program_id(2)
    blk = MOBA_BLOCK
    hpl = LANES // MOBA_DH

    @pl.when(i == 0)
    def _():
        for j in range(nblk):
            km_ref[j:j + 1, :] = jnp.mean(k_ref[0, j * blk:(j + 1) * blk, :], axis=0, keepdims=True)

    q = q_ref[0]
    jrow = lax.broadcasted_iota(I32, (nblk, blk), 0)
    rowi = lax.broadcasted_iota(I32, (blk, blk), 0)
    coli = lax.broadcasted_iota(I32, (blk, blk), 1)
    lane_q = lax.broadcasted_iota(I32, (blk, LANES), 1)
    lane_m = lax.broadcasted_iota(I32, (nblk, LANES), 1)
    blk_m = lax.broadcasted_iota(I32, (nblk, LANES), 0)
    off = pl.multiple_of(i * blk, blk)
    km = km_ref[...]
    own = [(lane_q // MOBA_DH) == hh for hh in range(hpl)]
    q_own = [jnp.where(own[hh], q, jnp.zeros_like(q)) for hh in range(hpl)]
    onehot = [(lane_m == blk_m + (hpl - 1 - hh) * MOBA_DH).astype(BF16) for hh in range(hpl)]
    q_aug = []

    def widen(x):
        return jnp.concatenate([x, x], axis=1)

    def score_tiles(tiles):
        scores = []
        for hh in range(hpl):
            row = []
            for kt, _, j in tiles:
                if j is None:
                    sj = _dot_nt(q_own[hh], kt)
                    sj = jnp.where(coli <= rowi, sj, NEG)
                else:
                    mark = jnp.broadcast_to(onehot[hh][j:j + 1, :], (blk, LANES))
                    sj = _dot_nt(q_aug[hh], jnp.where(own[hh], kt, mark))
                row.append(sj)
            scores.append(row)
        return scores

    def merge(tiles, scores, first):
        m_news, alphas = [], []
        for hh in range(hpl):
            top = functools.reduce(jnp.maximum, scores[hh])
            m_grp = jnp.max(top, axis=1, keepdims=True)
            if first:
                m_news.append(jnp.broadcast_to(m_grp, (blk, LANES)))
                alphas.append(None)
            else:
                m_old = m_ref[hh]
                m_new = jnp.maximum(m_old, m_grp)
                m_news.append(m_new)
                alphas.append(jnp.exp(m_old - m_new))
        probs = [[jnp.exp(sj - widen(m_news[hh])) for sj in scores[hh]] for hh in range(hpl)]
        pvs = [[_dot(pj.astype(BF16), vt) for pj, (_, vt, _) in zip(probs[hh], tiles)] for hh in range(hpl)]
        for hh in range(hpl):
            l_grp = jnp.sum(functools.reduce(lambda a, b: a + b, probs[hh]), axis=1, keepdims=True)
            acc = functools.reduce(lambda a, b: a + b, pvs[hh])
            if first:
                l_ref[hh] = jnp.broadcast_to(l_grp, (blk, LANES))
                acc_ref[hh] = acc
            else:
                l_ref[hh] = alphas[hh] * l_ref[hh] + l_grp
                acc_ref[hh] = alphas[hh] * acc_ref[hh] + acc
            m_ref[hh] = m_news[hh]

    gts = []
    for hh in range(hpl):
        k1, k2, k3 = _split3(jnp.where((lane_m // MOBA_DH) == hh, km, 0.0))
        gts.append(_dot_nt(k1, q) + _dot_nt(k2, q) + _dot_nt(k3, q))
    own_tiles = [(k_ref[0, pl.ds(off, blk), :].astype(BF16), v_ref[0, pl.ds(off, blk), :].astype(BF16), None)]
    own_scores = score_tiles(own_tiles)

    pen_ts = []
    for hh in range(hpl):
        gm = jnp.where(jrow < i, gts[hh], NEG)
        cnt = jnp.zeros((nblk, blk), I32)
        for jp in range(nblk):
            gj = gm[jp:jp + 1, :]
            beats = (gj > gm) | ((gj == gm) & (jp < jrow))
            cnt = cnt + beats.astype(I32)
        picked = (cnt < MOBA_TOPK) & (jrow < i)
        pen_ts.append(jnp.where(picked, 0.0, NEG).astype(BF16))
    for hh in range(hpl):
        pen = _dot_tn(pen_ts[hh], onehot[hh]).astype(BF16)
        q_aug.append(jnp.where(own[hh], q, pen))

    merge(own_tiles, own_scores, True)

    for j0 in range(0, nblk - 1, MOBA_GROUP):
        @pl.when(i > j0)
        def _(j0=j0):
            tiles = [(k_ref[0, j * blk:(j + 1) * blk, :].astype(BF16),
                      v_ref[0, j * blk:(j + 1) * blk, :].astype(BF16), j)
                     for j in range(j0, min(j0 + MOBA_GROUP, nblk - 1))]
            merge(tiles, score_tiles(tiles), False)

    o_ref[0] = jnp.where(own[0], acc_ref[0] / l_ref[0], acc_ref[1] / l_ref[1])


def _moba_prompt(q, k, v):
    b, s, w = q.shape
    assert s % MOBA_BLOCK == 0 and w == MOBA_W
    nblk = s // MOBA_BLOCK
    assert nblk % SUBLANES == 0 or nblk < SUBLANES
    hpl = LANES // MOBA_DH
    assert hpl == 2
    kv_spec = pl.BlockSpec((1, s, LANES), lambda bi, hp, i: (bi, 0, hp))
    q_spec = pl.BlockSpec((1, MOBA_BLOCK, LANES), lambda bi, hp, i: (bi, i, hp))
    return pl.pallas_call(
        functools.partial(_moba_prompt_kernel, nblk=nblk),
        grid=(b, w // LANES, nblk),
        in_specs=[q_spec, kv_spec, kv_spec],
        out_specs=q_spec,
        out_shape=jax.ShapeDtypeStruct((b, s, w), F32),
        scratch_shapes=[pltpu.VMEM((nblk, LANES), F32)]
        + [pltpu.VMEM((hpl, MOBA_BLOCK, LANES), F32)] * 3,
        compiler_params=_params(("arbitrary", "arbitrary", "arbitrary")),
        name="moba_prompt",
    )(q, k, v)


def _gla_consts(c):
    nlev = int(math.log2(c))
    assert 2 ** nlev == c
    r = np.arange(c)
    tri = (r[:, None] >= r[None, :]).astype(np.float32)
    sel = np.zeros((nlev, c, c), np.float32)
    msk = np.zeros((nlev + 1, c, c), np.float32)
    for l in range(nlev):
        b = 2 ** l
        ref_row = (r // (2 * b)) * (2 * b) + b - 1
        sel[l, r, ref_row] = 1.0
        same = (r[:, None] // (2 * b)) == (r[None, :] // (2 * b))
        msk[l] = same & ((r[:, None] % (2 * b)) >= b) & ((r[None, :] % (2 * b)) < b)
    msk[nlev] = np.eye(c)
    return jnp.asarray(tri, BF16), jnp.asarray(sel, BF16), jnp.asarray(msk, F32)


def _gla_kernel(q_ref, k_ref, v_ref, la_ref, s0_ref, tri_ref, sel_ref, msk_ref, o_ref, so_ref, st_ref,
                *, c, nchunk, nlev, t_real, nsteps):
    step = pl.program_id(1)

    @pl.when(step == 0)
    def _():
        st_ref[...] = s0_ref[0]

    eye = (lax.broadcasted_iota(I32, (GLA_DK, GLA_DK), 0)
           == lax.broadcasted_iota(I32, (GLA_DK, GLA_DK), 1)).astype(F32)
    chunks = range(nchunk)
    heads = range(GLA_HEADS)
    hsl = [slice(h * GLA_DK, (h + 1) * GLA_DK) for h in heads]
    vsl = [slice(h * GLA_DV, (h + 1) * GLA_DV) for h in heads]

    q, k, v, la = [], [], [], []
    for ci in chunks:
        rows = slice(ci * c, (ci + 1) * c)
        qc, kc, vc, lc = q_ref[rows, :], k_ref[rows, :], v_ref[rows, :], la_ref[rows, :]
        if t_real < c:
            live = lax.broadcasted_iota(I32, (c, 1), 0) < t_real
            lc = jnp.where(live, lc, 0.0)
            kc = jnp.where(live, kc, 0.0)
        q.append(qc); k.append(kc); v.append(vc.astype(BF16)); la.append(lc)
    a = [_dot_exact_lhs(tri_ref[...], la[ci]) for ci in chunks]
    arefs = [[_dot_exact_lhs(sel_ref[l], a[ci]) for l in range(nlev)] for ci in chunks]
    qb = [q[ci].astype(BF16) for ci in chunks]
    kb = [k[ci].astype(BF16) for ci in chunks]
    qt = [[(q[ci] * jnp.exp(jnp.minimum(a[ci] - arefs[ci][l], 0.0))).astype(BF16) for l in range(nlev)]
          for ci in chunks]
    kt = [[(k[ci] * jnp.exp(jnp.minimum(arefs[ci][l] - a[ci], 0.0))).astype(BF16) for l in range(nlev)]
          for ci in chunks]
    att = [[[_dot_nt(qb[ci][:, hsl[h]], kb[ci][:, hsl[h]])]
            + [_dot_nt(qt[ci][l][:, hsl[h]], kt[ci][l][:, hsl[h]]) for l in range(nlev)]
            for h in heads] for ci in chunks]
    att = [[(msk_ref[nlev] * att[ci][h][0]
             + functools.reduce(lambda x, y: x + y, [msk_ref[l] * att[ci][h][l + 1] for l in range(nlev)])
             ).astype(BF16) for h in heads] for ci in chunks]
    o_in = [[_dot(att[ci][h], v[ci][:, vsl[h]]) for h in heads] for ci in chunks]
    a_last = [a[ci][c - 1:c, :] for ci in chunks]
    qe = [(q[ci] * jnp.exp(a[ci])).astype(BF16) for ci in chunks]
    kd = [(k[ci] * jnp.exp(a_last[ci] - a[ci])).astype(BF16) for ci in chunks]
    e_last = [jnp.exp(a_last[ci]) for ci in chunks]
    kv = [[_dot_tn(kd[ci][:, hsl[h]], v[ci][:, vsl[h]]) for h in heads] for ci in chunks]
    e_col = [[jnp.sum(eye * e_last[ci][:, hsl[h]], axis=1, keepdims=True) for h in heads] for ci in chunks]
    for ci in chunks:
        sh = [st_ref[h] for h in heads]
        outs = [o_in[ci][h] + _dot(qe[ci][:, hsl[h]], sh[h].astype(BF16)) for h in heads]
        for h in heads:
            st_ref[h] = e_col[ci][h] * sh[h] + kv[ci][h]
        o_ref[ci * c:(ci + 1) * c, :] = jnp.concatenate(outs, axis=1)

    @pl.when(step == nsteps - 1)
    def _():
        so_ref[0] = st_ref[...]


def _gla(qg, kg, vg, la, s0, b, t, c, rows_per_step, t_real):
    assert t % rows_per_step == 0 and rows_per_step % c == 0
    nsteps = t // rows_per_step
    nlev = int(math.log2(c))
    tri, sel, msk = _gla_consts(c)
    row = lambda w: pl.BlockSpec((rows_per_step, w), lambda bi, i: (bi * nsteps + i, 0))
    st = pl.BlockSpec((1, GLA_HEADS, GLA_DK, GLA_DV), lambda bi, i: (bi, 0, 0, 0))
    const2 = pl.BlockSpec(tri.shape, lambda bi, i: (0, 0))
    const3 = lambda a: pl.BlockSpec(a.shape, lambda bi, i: (0, 0, 0))
    return pl.pallas_call(
        functools.partial(_gla_kernel, c=c, nchunk=rows_per_step // c, nlev=nlev, t_real=t_real,
                          nsteps=nsteps),
        grid=(b, nsteps),
        in_specs=[row(GLA_KW), row(GLA_KW), row(GLA_W), row(GLA_KW), st, const2, const3(sel), const3(msk)],
        out_specs=[row(GLA_W), st],
        out_shape=[jax.ShapeDtypeStruct((b * t, GLA_W), F32),
                   jax.ShapeDtypeStruct((b, GLA_HEADS, GLA_DK, GLA_DV), F32)],
        scratch_shapes=[pltpu.VMEM((GLA_HEADS, GLA_DK, GLA_DV), F32)],
        compiler_params=_params(("arbitrary", "arbitrary")),
        name="gla",
    )(qg, kg, vg, la, s0, tri, sel, msk)


def _mixer_out_kernel(x_ref, om_ref, og_ref, rg_ref, gg_ref, wo_ref, gq_ref, wq_ref, x1_ref, qm_ref):
    og = og_ref[...]
    parts = []
    for h in range(GLA_HEADS):
        oh = og[:, h * GLA_DV:(h + 1) * GLA_DV]
        parts.append(_rms(oh, gg_ref[...]))
    rg = rg_ref[...]
    ogn = jnp.concatenate(parts, axis=1) * (rg * jax.nn.sigmoid(rg))
    cat = jnp.concatenate([om_ref[...], ogn], axis=1).astype(BF16)
    x1 = x_ref[...] + _dot(cat, wo_ref[...])
    x1_ref[...] = x1
    qm_ref[...] = _dot(_rms(x1, gq_ref[...]).astype(BF16), wq_ref[...]).astype(qm_ref.dtype)


def _mixer_out(x2d, o_moba, o_gla, rg, g_gla_out, w_out, g_mem_q, w_mem_q):
    n = x2d.shape[0]
    tm = _row_tile(n, 512)
    row = lambda w: pl.BlockSpec((tm, w), lambda i: (i, 0))
    full = lambda a: pl.BlockSpec(a.shape, lambda i: (0, 0))
    gg = g_gla_out.reshape(1, GLA_DV)
    gq = g_mem_q.reshape(1, D_MODEL)
    return pl.pallas_call(
        _mixer_out_kernel,
        grid=(n // tm,),
        in_specs=[row(D_MODEL), row(MOBA_W), row(GLA_W), row(GLA_W), full(gg), full(w_out), full(gq),
                  full(w_mem_q)],
        out_specs=[row(D_MODEL), row(D_MODEL)],
        out_shape=[jax.ShapeDtypeStruct((n, D_MODEL), F32), jax.ShapeDtypeStruct((n, D_MODEL), BF16)],
        compiler_params=_params(("arbitrary",)),
        name="mixer_out",
    )(x2d, o_moba, o_gla, rg, gg, w_out, gq, w_mem_q)


def _mem_kv_kernel(m_ref, g_ref, wk_ref, wv_ref, mk_ref, mv_ref):
    h = _rms(m_ref[...], g_ref[...]).astype(BF16)
    mk_ref[...] = _dot(h, wk_ref[...])
    mv_ref[...] = _dot(h, wv_ref[...])


def _mem_kv(mem2d, g_mem_kv, w_mem_k, w_mem_v):
    n = mem2d.shape[0]
    tm = _row_tile(n, 512)
    row = pl.BlockSpec((tm, D_MODEL), lambda i: (i, 0))
    full = lambda a: pl.BlockSpec(a.shape, lambda i: (0, 0))
    g = g_mem_kv.reshape(1, D_MODEL)
    return pl.pallas_call(
        _mem_kv_kernel,
        grid=(n // tm,),
        in_specs=[row, full(g), full(w_mem_k), full(w_mem_v)],
        out_specs=[row, row],
        out_shape=[jax.ShapeDtypeStruct((n, D_MODEL), F32)] * 2,
        compiler_params=_params(("arbitrary",)),
        name="memory_kv",
    )(mem2d, g, w_mem_k, w_mem_v)


def _mem_attend_kernel(q_ref, mk_ref, mv_ref, x1_ref, wo_ref, gf_ref, x2_ref, h3_ref):
    q = q_ref[...]
    mk = mk_ref[0].astype(BF16)
    mv = mv_ref[0].astype(BF16)
    hsl = [slice(h * MEM_DH, (h + 1) * MEM_DH) for h in range(MEM_HEADS)]
    scores = [_dot_nt(q[:, hs], mk[:, hs]) * (MEM_DH ** -0.5) for hs in hsl]
    probs = [jnp.exp(s - jnp.max(s, axis=1, keepdims=True)) for s in scores]
    pv = [_dot(p.astype(BF16), mv[:, hs]) for p, hs in zip(probs, hsl)]
    outs = [o / jnp.sum(p, axis=1, keepdims=True) for o, p in zip(pv, probs)]
    o = jnp.concatenate(outs, axis=1).astype(BF16)
    x2 = x1_ref[...] + _dot(o, wo_ref[...])
    x2_ref[...] = x2
    h3_ref[...] = _rms(x2, gf_ref[...])


def _mem_attend(qmem, mk, mv, x1, w_mem_o, g_ffn, b, t):
    tq = _row_tile(t, 512)
    nq = t // tq
    row = pl.BlockSpec((tq, D_MODEL), lambda bi, i: (bi * nq + i, 0))
    mem = pl.BlockSpec((1,) + mk.shape[1:], lambda bi, i: (bi, 0, 0))
    full = lambda a: pl.BlockSpec(a.shape, lambda bi, i: (0, 0))
    gf = g_ffn.reshape(1, D_MODEL)
    return pl.pallas_call(
        _mem_attend_kernel,
        grid=(b, nq),
        in_specs=[row, mem, mem, row, full(w_mem_o), full(gf)],
        out_specs=[row, row],
        out_shape=[jax.ShapeDtypeStruct((b * t, D_MODEL), F32)] * 2,
        compiler_params=_params(("arbitrary", "arbitrary")),
        name="memory_attend",
    )(qmem, mk, mv, x1, w_mem_o, gf)


def _peer_candidates():
    k = PEER_TOPK
    flat = []
    flat += [0 * k + j for j in range(k)]
    for i in range(1, 8):
        flat += [i * k + j if (i + 1) * (j + 1) <= k else -1 for j in range(8)]
    flat += [(8 + r) * k for r in range(8)]
    return np.asarray(flat, np.int32)


def _route_kernel(h_ref, wq_ref, keys_ref, flat_ref, eid_ref, g_ref):
    tq = h_ref.shape[0]
    k = PEER_TOPK
    hb = h_ref[...].astype(BF16)
    kio = lax.broadcasted_iota(I32, (PEER_NKEYS, tq), 0)
    r16 = lax.broadcasted_iota(I32, (k, tq), 0)
    flat = flat_ref[...]
    nexp = PEER_NKEYS * PEER_NKEYS
    big = jnp.int32(2 ** 30)
    qv_all = _dot(hb, wq_ref[...]).astype(BF16)

    def top16(s):
        def body(kk, carry):
            s, vals, idxs = carry
            m = jnp.max(s, axis=0, keepdims=True)
            idx = jnp.min(jnp.where(s == m, kio, PEER_NKEYS), axis=0, keepdims=True)
            vals = jnp.where(r16 == kk, m, vals)
            idxs = jnp.where(r16 == kk, idx, idxs)
            s = jnp.where(kio == idx, -jnp.inf, s)
            return s, vals, idxs
        _, vals, idxs = lax.fori_loop(0, k, body, (s, jnp.zeros((k, tq), F32), jnp.zeros((k, tq), I32)))
        return vals, idxs

    dq = PEER_DQ // 2
    scores = [_dot_nt(keys_ref[hp], qv_all[:, hp * dq:(hp + 1) * dq])
              for hp in range(2 * PEER_HEADS)]

    def candidates(h):
        tv, ti = [], []
        for p in range(2):
            vals, idxs = top16(scores[h * 2 + p])
            tv.append(vals)
            ti.append(idxs)
        a, b = tv
        ia, ib = ti
        cs = [a[0:1] + b]
        ce = [ia[0:1] * PEER_NKEYS + ib]
        for i in range(1, 8):
            cs.append(a[i:i + 1] + b[0:8])
            ce.append(ia[i:i + 1] * PEER_NKEYS + ib[0:8])
        cs.append(a[8:16] + b[0:1])
        ce.append(ia[8:16] * PEER_NKEYS + ib[0:1])
        cand = jnp.where(flat >= 0, jnp.concatenate(cs, axis=0), -jnp.inf)
        ckey = jnp.where(flat >= 0, flat * nexp + jnp.concatenate(ce, axis=0), big)
        return cand, ckey

    g_all, e_all = [], []
    for h0 in range(0, PEER_HEADS, 2):
        cands = [candidates(h0), candidates(h0 + 1)]

        def body2(kk, carry):
            out = []
            for (cand, vals, eids), (_, ckey) in zip((carry[:3], carry[3:]), cands):
                m = jnp.max(cand, axis=0, keepdims=True)
                pick = jnp.min(jnp.where(cand == m, ckey, big), axis=0, keepdims=True)
                vals = jnp.where(r16 == kk, m, vals)
                eids = jnp.where(r16 == kk, pick & (nexp - 1), eids)
                cand = jnp.where(ckey == pick, -jnp.inf, cand)
                out += [cand, vals, eids]
            return tuple(out)
        zf, zi = jnp.zeros((k, tq), F32), jnp.zeros((k, tq), I32)
        res = lax.fori_loop(0, k, body2, (cands[0][0], zf, zi, cands[1][0], zf, zi))
        for fin, eids in ((res[1], res[2]), (res[4], res[5])):
            ex = jnp.exp(fin - fin[0:1])
            g_all.append(ex / jnp.sum(ex, axis=0, keepdims=True))
            e_all.append(lax.bitcast_convert_type(eids, F32))
    g_ref[...] = jnp.concatenate(g_all, axis=0).T
    eid_ref[...] = lax.bitcast_convert_type(jnp.concatenate(e_all, axis=0).T, I32)


def _route(h3, wq_hp, keys_hp):
    n = h3.shape[0]
    tq = _row_tile(n)
    flat = jnp.broadcast_to(jnp.asarray(_peer_candidates())[:, None], (_peer_candidates().shape[0], tq))
    out = pl.BlockSpec((tq, PEER_PAIRS), lambda i: (i, 0))
    return pl.pallas_call(
        _route_kernel,
        grid=(n // tq,),
        in_specs=[pl.BlockSpec((tq, D_MODEL), lambda i: (i, 0)),
                  pl.BlockSpec(wq_hp.shape, lambda i: (0, 0)),
                  pl.BlockSpec(keys_hp.shape, lambda i: (0, 0, 0)),
                  pl.BlockSpec(flat.shape, lambda i: (0, 0))],
        out_specs=[out, out],
        out_shape=[jax.ShapeDtypeStruct((n, PEER_PAIRS), I32), jax.ShapeDtypeStruct((n, PEER_PAIRS), F32)],
        compiler_params=_params(("arbitrary",)),
        name="peer_route",
    )(h3, wq_hp, keys_hp, flat)


PEER_TT = 8
PEER_NBUF = 4
PEER_AHEAD = 2
PEER_TOK = 4
ROW_WORDS = D_MODEL // 2
ROW_SUB = ROW_WORDS // LANES
REC_SUB = 2 * ROW_SUB
assert REC_SUB == SUBLANES
REC_WORDS = REC_SUB * LANES
SC_WORKERS = 32
SC_CHUNK = 64
SC_TOKENS = 8192


PACK_ROWS = 256


def _pack_kernel(u_ref, v_ref, o_ref):
    def words(x):
        b = lax.bitcast_convert_type(x, I32)
        r = b + 0x7FFF + ((b >> 16) & 1)
        return ((r[:, :ROW_WORDS] >> 16) & 0xFFFF) | (r[:, ROW_WORDS:] & jnp.int32(-65536))
    wu, wv = words(u_ref[0]), words(v_ref[0])
    n = u_ref.shape[1]
    for s in range(ROW_SUB):
        o_ref[pl.ds(s, n, stride=REC_SUB), :] = wu[:, s * LANES:(s + 1) * LANES]
        o_ref[pl.ds(ROW_SUB + s, n, stride=REC_SUB), :] = wv[:, s * LANES:(s + 1) * LANES]


def _pack_experts(u, v, layer):
    e = u.shape[1]
    eb = PACK_ROWS if e % PACK_ROWS == 0 else e
    row = pl.BlockSpec((1, eb, D_MODEL), lambda i: (layer, i, 0))
    return pl.pallas_call(
        _pack_kernel,
        grid=(e // eb,),
        in_specs=[row, row],
        out_specs=pl.BlockSpec((eb * REC_SUB, LANES), lambda i: (i, 0)),
        out_shape=jax.ShapeDtypeStruct((e * REC_SUB, LANES), I32),
        compiler_params=_params(("arbitrary",)),
        name="pack_experts",
    )(u, v)


def _unpack(w):
    lo = lax.bitcast_convert_type(w << 16, F32)
    hi = lax.bitcast_convert_type(w & jnp.int32(-65536), F32)
    return lo, hi


def _peer_kernel(eid_ref, eidn_ref, g_ref, h_ref, x2_ref, gf_ref, tab_ref, y_ref, *rest, nsteps):
    bufs, sem, po_ref = rest[:PEER_NBUF], rest[PEER_NBUF], rest[PEER_NBUF + 1]
    i = pl.program_id(0)
    tile_rows = PEER_TT * PEER_PAIRS * REC_SUB

    def gather(eref, row0, k, t):
        for p in range(PEER_PAIRS):
            src = pl.multiple_of(eref[row0 + t, p] * REC_SUB, REC_SUB)
            dst = pl.multiple_of((t * PEER_PAIRS + p) * REC_SUB, REC_SUB)
            pltpu.make_async_copy(tab_ref.at[pl.ds(src, REC_SUB), :],
                                  bufs[k].at[pl.ds(dst, REC_SUB), :], sem.at[k]).start(priority=p % 2)

    def wait_tile(k):
        pltpu.make_async_copy(tab_ref.at[pl.ds(0, tile_rows), :], bufs[k], sem.at[k]).wait()

    eye = (lax.broadcasted_iota(I32, (PEER_PAIRS, PEER_PAIRS), 0)
           == lax.broadcasted_iota(I32, (PEER_PAIRS, PEER_PAIRS), 1)).astype(F32)

    def consume(k, t):
        buf = bufs[k]
        row = k * PEER_TT + t
        base = t * (PEER_PAIRS * REC_SUB)
        x = h_ref[pl.ds(row, 1), :]
        dacc = jnp.zeros((PEER_PAIRS, LANES), F32)
        for s in range(ROW_SUB):
            lo, hi = _unpack(buf[pl.ds(base + s, PEER_PAIRS, stride=REC_SUB), :])
            dacc = dacc + lo * x[:, s * LANES:(s + 1) * LANES] \
                + hi * x[:, ROW_WORDS + s * LANES:ROW_WORDS + (s + 1) * LANES]
        d = jnp.sum(dacc, axis=1, keepdims=True)
        act = 0.5 * d * (1.0 + lax.erf(d * (2.0 ** -0.5)))
        gcol = jnp.sum(eye * g_ref[pl.ds(row, 1), :], axis=1, keepdims=True)
        wcol = gcol * act
        los, his = [], []
        for s in range(ROW_SUB, REC_SUB):
            lo, hi = _unpack(buf[pl.ds(base + s, PEER_PAIRS, stride=REC_SUB), :])
            los.append(jnp.sum(wcol * lo, axis=0, keepdims=True))
            his.append(jnp.sum(wcol * hi, axis=0, keepdims=True))
        po_ref[pl.ds(row, 1), :] = jnp.concatenate(los + his, axis=1)

    @pl.when(i == 0)
    def _():
        for k in range(PEER_AHEAD):
            def first(t, c, k=k):
                gather(eid_ref, k * PEER_TT, k, t)
                return c
            lax.fori_loop(0, PEER_TT, first, 0)

    def phase(tt, c, k, prefetch):
        for t in [tt * PEER_TOK + j for j in range(PEER_TOK)]:
            if prefetch:
                ahead = k + PEER_AHEAD
                if ahead < PEER_NBUF:
                    gather(eid_ref, ahead * PEER_TT, ahead, t)
                else:
                    gather(eidn_ref, (ahead - PEER_NBUF) * PEER_TT, ahead - PEER_NBUF, t)
        for t in [tt * PEER_TOK + j for j in range(PEER_TOK)]:
            consume(k, t)
        return c

    trips = PEER_TT // PEER_TOK
    n_next = jnp.where(i + 1 < nsteps, trips, 0)
    for k in range(PEER_NBUF):
        wait_tile(k)
        n_pre = trips if k + PEER_AHEAD < PEER_NBUF else n_next
        lax.fori_loop(0, n_pre, functools.partial(phase, k=k, prefetch=True), 0)
        if k + PEER_AHEAD >= PEER_NBUF:
            lax.fori_loop(n_pre, trips, functools.partial(phase, k=k, prefetch=False), 0)
    y_ref[...] = _rms(x2_ref[...] + po_ref[...], gf_ref[...])


def _peer_mix(eid, g, h3, x2, g_final, table):
    n = h3.shape[0]
    rows = PEER_NBUF * PEER_TT
    assert n % rows == 0
    nsteps = n // rows
    row = lambda w: pl.BlockSpec((rows, w), lambda i: (i, 0))
    gf = g_final.reshape(1, D_MODEL)
    buf = pltpu.VMEM((PEER_TT * PEER_PAIRS * REC_SUB, LANES), I32)
    return pl.pallas_call(
        functools.partial(_peer_kernel, nsteps=nsteps),
        grid=(nsteps,),
        in_specs=[pl.BlockSpec((rows, PEER_PAIRS), lambda i: (i, 0), memory_space=pltpu.SMEM),
                  pl.BlockSpec((rows, PEER_PAIRS), lambda i: (jnp.minimum(i + 1, nsteps - 1), 0),
                               memory_space=pltpu.SMEM),
                  row(PEER_PAIRS), row(D_MODEL), row(D_MODEL),
                  pl.BlockSpec((1, D_MODEL), lambda i: (0, 0)),
                  pl.BlockSpec(memory_space=pl.ANY)],
        out_specs=row(D_MODEL),
        out_shape=jax.ShapeDtypeStruct((n, D_MODEL), F32),
        scratch_shapes=[buf] * PEER_NBUF + [pltpu.SemaphoreType.DMA((PEER_NBUF,)),
                                            pltpu.VMEM((rows, D_MODEL), F32)],
        compiler_params=_params(("arbitrary",)),
        name="peer_mix",
    )(eid, eid, g, h3, x2, gf, table)


def _sc_gather_records(table_rows, idx):
    n = idx.shape[0]
    per_w = n // SC_WORKERS
    assert n % (SC_WORKERS * SC_CHUNK) == 0
    mesh = plsc.VectorSubcoreMesh(core_axis_name="c", subcore_axis_name="s")

    @functools.partial(
        pl.kernel, mesh=mesh,
        out_type=jax.ShapeDtypeStruct((n, REC_WORDS), I32),
        scratch_types=[pltpu.VMEM((SC_CHUNK,), I32), pltpu.VMEM((SC_CHUNK, REC_WORDS), I32),
                       pltpu.SemaphoreType.DMA],
    )
    def gather(table_hbm, idx_hbm, out_hbm, idx_v, rows_v, sem):
        wid = lax.axis_index("s") * 2 + lax.axis_index("c")
        base = wid * per_w

        @pl.loop(0, per_w // SC_CHUNK)
        def _(ch):
            off = pl.multiple_of(base + ch * SC_CHUNK, SC_CHUNK)
            pltpu.sync_copy(idx_hbm.at[pl.ds(off, SC_CHUNK)], idx_v)
            pltpu.async_copy(table_hbm.at[idx_v], rows_v, sem).wait()
            pltpu.sync_copy(rows_v, out_hbm.at[pl.ds(off, SC_CHUNK)])

    return gather(table_rows, idx)


def _peer_staged_kernel(rec_ref, g_ref, h_ref, x2_ref, gf_ref, y_ref, po_ref):
    eye = (lax.broadcasted_iota(I32, (PEER_PAIRS, PEER_PAIRS), 0)
           == lax.broadcasted_iota(I32, (PEER_PAIRS, PEER_PAIRS), 1)).astype(F32)

    def token(t, c):
        r0 = pl.multiple_of(t * PEER_PAIRS, PEER_PAIRS)
        x = h_ref[pl.ds(t, 1), :]
        dacc = jnp.zeros((PEER_PAIRS, LANES), F32)
        for s in range(ROW_SUB):
            lo, hi = _unpack(rec_ref[pl.ds(r0, PEER_PAIRS), s * LANES:(s + 1) * LANES])
            dacc = dacc + lo * x[:, s * LANES:(s + 1) * LANES] \
                + hi * x[:, ROW_WORDS + s * LANES:ROW_WORDS + (s + 1) * LANES]
        d = jnp.sum(dacc, axis=1, keepdims=True)
        act = 0.5 * d * (1.0 + lax.erf(d * (2.0 ** -0.5)))
        gcol = jnp.sum(eye * g_ref[pl.ds(t, 1), :], axis=1, keepdims=True)
        wcol = gcol * act
        los, his = [], []
        for s in range(ROW_SUB, REC_SUB):
            lo, hi = _unpack(rec_ref[pl.ds(r0, PEER_PAIRS), s * LANES:(s + 1) * LANES])
            los.append(jnp.sum(wcol * lo, axis=0, keepdims=True))
            his.append(jnp.sum(wcol * hi, axis=0, keepdims=True))
        po_ref[pl.ds(t, 1), :] = jnp.concatenate(los + his, axis=1)
        return c

    lax.fori_loop(0, PEER_TT, token, 0)
    y_ref[...] = _rms(x2_ref[...] + po_ref[...], gf_ref[...])


def _peer_staged(records, g, h3, x2, g_final):
    n = h3.shape[0]
    assert n % PEER_TT == 0
    row = lambda w: pl.BlockSpec((PEER_TT, w), lambda i: (i, 0))
    gf = g_final.reshape(1, D_MODEL)
    return pl.pallas_call(
        _peer_staged_kernel,
        grid=(n // PEER_TT,),
        in_specs=[pl.BlockSpec((PEER_TT * PEER_PAIRS, REC_WORDS), lambda i: (i, 0)),
                  row(PEER_PAIRS), row(D_MODEL), row(D_MODEL), pl.BlockSpec((1, D_MODEL), lambda i: (0, 0))],
        out_specs=row(D_MODEL),
        out_shape=jax.ShapeDtypeStruct((n, D_MODEL), F32),
        scratch_shapes=[pltpu.VMEM((PEER_TT, D_MODEL), F32)],
        compiler_params=_params(("arbitrary",)),
        name="peer_staged",
    )(records, g, h3, x2, gf)


def _peer(eid, g, h3, x2, g_final, table):
    n = h3.shape[0]
    if n < 2 * SC_TOKENS:
        return _peer_mix(eid, g, h3, x2, g_final, table)
    n1 = n - SC_TOKENS
    e = table.shape[0] // REC_SUB
    records = _sc_gather_records(table.reshape(e, REC_WORDS), eid[n1:].reshape(SC_TOKENS * PEER_PAIRS))
    y1 = _peer_mix(eid[:n1], g[:n1], h3[:n1], x2[:n1], g_final, table)
    y2 = _peer_staged(records, g[n1:], h3[n1:], x2[n1:], g_final)
    return jnp.concatenate([y1, y2], axis=0)


def _page_sum_kernel(k_ref, o_ref):
    pb = k_ref.shape[0]
    hpl = LANES // MOBA_DH
    d_i = lax.broadcasted_iota(I32, (MOBA_DH, LANES), 0)
    l_i = lax.broadcasted_iota(I32, (MOBA_DH, LANES), 1)
    place = [(l_i == d_i + hh * MOBA_DH).astype(F32) for hh in range(hpl)]

    def one_page(pg, c):
        rows = []
        for hp in range(MOBA_HEADS // hpl):
            row = jnp.zeros((1, LANES), F32)
            for hh in range(hpl):
                r = jnp.sum(k_ref[pg, hp * hpl + hh], axis=1, keepdims=True)
                row = row + jnp.sum(r * place[hh], axis=0, keepdims=True)
            rows.append(row)
        o_ref[pl.ds(pg, 1), :] = jnp.concatenate(rows, axis=1)
        return c

    lax.fori_loop(0, pb, one_page, 0, unroll=4)


def _page_sums(cache_kt):
    p, nh, dh, page = cache_kt.shape
    pb = max(d for d in range(1, 17) if p % d == 0 and (d % SUBLANES == 0 or d == p))
    return pl.pallas_call(
        _page_sum_kernel,
        grid=(p // pb,),
        in_specs=[pl.BlockSpec((pb, nh, dh, page), lambda i: (i, 0, 0, 0))],
        out_specs=pl.BlockSpec((pb, nh * dh), lambda i: (i, 0)),
        out_shape=jax.ShapeDtypeStruct((p, nh * dh), F32),
        compiler_params=_params(("arbitrary",)),
        name="page_sums",
    )(cache_kt)


def _sample_gate_kernel(pt_ref, ps_ref, q_ref, sel_ref, km_ref, *, nfull, ppb):
    b = pl.program_id(0)
    for j in range(nfull):
        acc = ps_ref[pl.ds(pt_ref[b, j * ppb], 1), :]
        for pp in range(1, ppb):
            acc = acc + ps_ref[pl.ds(pt_ref[b, j * ppb + pp], 1), :]
        km_ref[j:j + 1, :] = acc * (1.0 / MOBA_BLOCK)
    q = q_ref[...].astype(BF16)
    lane = lax.broadcasted_iota(I32, (SAMPLE_TPAD, nfull), 1)
    olane = lax.broadcasted_iota(I32, (SAMPLE_TPAD, LANES), 1)
    out = jnp.zeros((SAMPLE_TPAD, LANES), I32)
    for h in range(MOBA_HEADS):
        ls = slice(h * MOBA_DH, (h + 1) * MOBA_DH)
        k1, k2, k3 = _split3(km_ref[:, ls])
        qh = q[:, ls]
        g = _dot_nt(qh, k1) + _dot_nt(qh, k2) + _dot_nt(qh, k3)
        for kk in range(MOBA_TOPK):
            m = jnp.max(g, axis=1, keepdims=True)
            idx = jnp.min(jnp.where(g == m, lane, nfull), axis=1, keepdims=True)
            out = out + jnp.where(olane == h * MOBA_TOPK + kk, idx, 0)
            g = jnp.where(lane == idx, -jnp.inf, g)
    sel_ref[...] = out


def _sample_gate(page_table, page_sum, q_s, nfull, ppb):
    db = page_table.shape[0]
    return pl.pallas_call(
        functools.partial(_sample_gate_kernel, nfull=nfull, ppb=ppb),
        grid_spec=pltpu.PrefetchScalarGridSpec(
            num_scalar_prefetch=1,
            grid=(db,),
            in_specs=[pl.BlockSpec(page_sum.shape, lambda b, pt: (0, 0)),
                      pl.BlockSpec((SAMPLE_TPAD, MOBA_W), lambda b, pt: (b, 0))],
            out_specs=pl.BlockSpec((SAMPLE_TPAD, LANES), lambda b, pt: (b, 0)),
            scratch_shapes=[pltpu.VMEM((nfull, MOBA_W), F32)]),
        out_shape=jax.ShapeDtypeStruct((db * SAMPLE_TPAD, LANES), I32),
        compiler_params=_params(("arbitrary",)),
        name="sample_gate",
    )(page_table, page_sum, q_s)


def _sample_attn_kernel(pt_ref, sel_ref, q_ref, kn_ref, vn_ref, ck_ref, cv_ref, o_ref, kbuf, vbuf, sem,
                        *, t_real, ppb, page, nsteps):
    b = pl.program_id(0)
    hp = pl.program_id(1)
    nhp = MOBA_HEADS // (LANES // MOBA_DH)
    hpl = LANES // MOBA_DH
    nslab = t_real * hpl * MOBA_TOPK * ppb
    step = b * nhp + hp
    slot = step % 2

    def copies(bb, hpp, sl, n):
        t, r = divmod(n, hpl * MOBA_TOPK * ppb)
        hh, r = divmod(r, MOBA_TOPK * ppb)
        kk, pp = divmod(r, ppb)
        h = hpp * hpl + hh
        blk = sel_ref[bb, (t * MOBA_HEADS + h) * MOBA_TOPK + kk]
        phys = pt_ref[bb, blk * ppb + pp]
        return (pltpu.make_async_copy(ck_ref.at[phys, h], kbuf.at[sl, n], sem.at[0, sl]),
                pltpu.make_async_copy(cv_ref.at[phys, h], vbuf.at[sl, n], sem.at[1, sl]))

    def fetch(bb, hpp, sl):
        for n in range(nslab):
            ck, cv = copies(bb, hpp, sl, n)
            ck.start()
            cv.start()

    @pl.when(step == 0)
    def _():
        fetch(b, hp, slot)

    @pl.when(step + 1 < nsteps)
    def _():
        nxt = step + 1
        fetch(nxt // nhp, nxt % nhp, 1 - slot)

    for n in range(nslab):
        ck, cv = copies(b, hp, slot, n)
        ck.wait()
        cv.wait()

    q = q_ref[...].astype(BF16)
    kn = kn_ref[...].astype(BF16)
    vn = vn_ref[...].astype(BF16)
    rows = lax.broadcasted_iota(I32, (SAMPLE_TPAD, SAMPLE_TPAD), 0)
    cols = lax.broadcasted_iota(I32, (SAMPLE_TPAD, SAMPLE_TPAD), 1)
    rowc = lax.broadcasted_iota(I32, (SAMPLE_TPAD, 1), 0)
    nsel = MOBA_TOPK * ppb
    lsl = [slice(hh * MOBA_DH, (hh + 1) * MOBA_DH) for hh in range(hpl)]
    pairs = [(hh, t) for hh in range(hpl) for t in range(t_real)]
    s_new = [jnp.where((cols <= rows) & (cols < t_real), _dot_nt(q[:, ls], kn[:, ls]), NEG) for ls in lsl]
    kc, vc = {}, {}
    for hh, t in pairs:
        n0 = (t * hpl + hh) * nsel
        kc[hh, t] = jnp.concatenate([kbuf[slot, n0 + r] for r in range(nsel)], axis=1).astype(BF16)
        vc[hh, t] = jnp.concatenate([vbuf[slot, n0 + r] for r in range(nsel)], axis=1).astype(BF16)
    sc = {ht: _dot(q[:, lsl[ht[0]]], kc[ht]) for ht in pairs}
    m = {ht: jnp.maximum(jnp.max(sc[ht], axis=1, keepdims=True), jnp.max(s_new[ht[0]], axis=1, keepdims=True))
         for ht in pairs}
    p = {ht: jnp.exp(sc[ht] - m[ht]) for ht in pairs}
    pn = {ht: jnp.exp(s_new[ht[0]] - m[ht]) for ht in pairs}
    o = {ht: _dot_nt(p[ht].astype(BF16), vc[ht]) + _dot(pn[ht].astype(BF16), vn[:, lsl[ht[0]]]) for ht in pairs}
    outs = []
    for hh in range(hpl):
        acc = jnp.zeros((SAMPLE_TPAD, MOBA_DH), F32)
        for t in range(t_real):
            l = jnp.sum(p[hh, t], axis=1, keepdims=True) + jnp.sum(pn[hh, t], axis=1, keepdims=True)
            acc = acc + jnp.where(rowc == t, o[hh, t] / l, 0.0)
        outs.append(acc)
    o_ref[...] = jnp.concatenate(outs, axis=1)


def _sample_attn(page_table, sel, q_s, k_new, v_new, cache_kt, cache_vt, t_real, ppb):
    db = page_table.shape[0]
    page = cache_kt.shape[3]
    hpl = LANES // MOBA_DH
    nslab = t_real * hpl * MOBA_TOPK * ppb
    row = pl.BlockSpec((SAMPLE_TPAD, LANES), lambda b, hp, pt, sl: (b, hp))
    anyspec = pl.BlockSpec(memory_space=pl.ANY)
    return pl.pallas_call(
        functools.partial(_sample_attn_kernel, t_real=t_real, ppb=ppb, page=page,
                          nsteps=db * (MOBA_W // LANES)),
        grid_spec=pltpu.PrefetchScalarGridSpec(
            num_scalar_prefetch=2,
            grid=(db, MOBA_W // LANES),
            in_specs=[row, row, row, anyspec, anyspec],
            out_specs=row,
            scratch_shapes=[pltpu.VMEM((2, nslab, MOBA_DH, page), F32),
                            pltpu.VMEM((2, nslab, MOBA_DH, page), F32),
                            pltpu.SemaphoreType.DMA((2, 2))]),
        out_shape=jax.ShapeDtypeStruct((db * SAMPLE_TPAD, MOBA_W), F32),
        compiler_params=_params(("arbitrary", "arbitrary")),
        name="sample_attn",
    )(page_table, sel, q_s, k_new, v_new, cache_kt, cache_vt)


def _layer_tail(x2d, o_moba, o_gla, rg, b, t, mk, mv, wts, live_rows, g_last):
    x1, qmem = _mixer_out(x2d, o_moba, o_gla, rg, wts["g_gla_out"], wts["w_out"], wts["g_mem_q"], wts["w_mem_q"])
    x2, h3 = _mem_attend(qmem, mk, mv, x1, wts["w_mem_o"], wts["g_ffn"], b, t)
    if live_rows is not None:
        x2 = x2.reshape(b, t, D_MODEL)[:, :live_rows].reshape(b * live_rows, D_MODEL)
        h3 = h3.reshape(b, t, D_MODEL)[:, :live_rows].reshape(b * live_rows, D_MODEL)
    eid, g = _route(h3, wts["wq_hp"], wts["keys_hp"])
    return _peer(eid, g, h3, x2, g_last, wts["table"])


def kernel(x_prompt, x_sample, cache_k, cache_v, state_gla, cache_mem_k, cache_mem_v, page_table, mem_prompt,
           g_mix, w_in, w_alpha_up, b_alpha, g_gla_out, w_out, g_mem_q, g_mem_kv, w_mem_q, w_mem_k, w_mem_v,
           w_mem_o, g_ffn, w_peer_q, peer_sub_keys, expert_u, expert_v, g_final):
    depth = w_in.shape[0]
    assert depth == 1, "the final norm is fused into the last layer's PEER kernel"
    b, s, _ = x_prompt.shape
    db, t, _ = x_sample.shape
    n_pages = page_table.shape[1]
    page = cache_k.shape[2]
    past = n_pages * page
    ppb = MOBA_BLOCK // page
    nfull = past // MOBA_BLOCK
    assert past % MOBA_BLOCK == 0 and t <= SAMPLE_TPAD and t <= MOBA_BLOCK and nfull >= MOBA_TOPK
    mem_len = mem_prompt.shape[1]

    l = 0
    wts = dict(
        g_gla_out=g_gla_out[l], w_out=w_out[l].astype(BF16), g_mem_q=g_mem_q[l],
        w_mem_q=w_mem_q[l].astype(BF16), w_mem_o=w_mem_o[l].astype(BF16), g_ffn=g_ffn[l],
        wq_hp=w_peer_q[l].astype(BF16),
        keys_hp=peer_sub_keys[l].astype(BF16).transpose(1, 0, 2, 3).reshape(
            PEER_HEADS * 2, PEER_NKEYS, PEER_DQ // 2),
        table=_pack_experts(expert_u, expert_v, l),
    )
    w_in_p = jnp.pad(w_in[l], ((0, 0), (0, IN_COLS_PAD - IN_COLS))).astype(BF16)
    wa_p = jnp.pad(w_alpha_up[l], ((0, LANES - GLA_LOWRANK), (0, 0))).astype(BF16)

    xp = x_prompt.reshape(b * s, D_MODEL)
    q, k, v, qg, kg, vg, rg, la = _mixer_inputs(xp, jnp.arange(s), g_mix[l], w_in_p, wa_p, b_alpha[l], BF16)
    o_moba = _moba_prompt(q.reshape(b, s, MOBA_W), k.reshape(b, s, MOBA_W), v.reshape(b, s, MOBA_W))
    c = GLA_CHUNK if s % GLA_CHUNK == 0 else s
    o_gla, st_p = _gla(qg, kg, vg, la, jnp.zeros((b, GLA_HEADS, GLA_DK, GLA_DV), F32), b, s, c,
                       _row_tile(s), c)
    mk_p, mv_p = _mem_kv(mem_prompt.reshape(b * mem_len, D_MODEL), g_mem_kv[l],
                         w_mem_k[l].astype(BF16), w_mem_v[l].astype(BF16))
    y_p = _layer_tail(xp, o_moba.reshape(b * s, MOBA_W), o_gla, rg, b, s,
                      mk_p.reshape(b, mem_len, D_MODEL), mv_p.reshape(b, mem_len, D_MODEL), wts, None, g_final)

    tp = SAMPLE_TPAD
    xs = jnp.pad(x_sample, ((0, 0), (0, tp - t), (0, 0))).reshape(db * tp, D_MODEL)
    pos_s = past + (jnp.arange(db * tp) % tp)
    qs, ks, vs, qgs, kgs, vgs, rgs, las = _mixer_inputs(xs, pos_s, g_mix[l], w_in_p, wa_p, b_alpha[l], F32)
    ckt = cache_k[l].transpose(0, 2, 3, 1)
    cvt = cache_v[l].transpose(0, 2, 3, 1)
    sel = _sample_gate(page_table, _page_sums(ckt), qs, nfull, ppb)
    sel = sel.reshape(db, tp, LANES)[:, :t, :MOBA_HEADS * MOBA_TOPK].reshape(db, t * MOBA_HEADS * MOBA_TOPK)
    o_moba_s = _sample_attn(page_table, sel, qs, ks, vs, ckt, cvt, t, ppb)
    o_gla_s, st_s = _gla(qgs, kgs, vgs, las, state_gla[l], db, tp, tp, tp, t)
    mem_s = cache_mem_k.shape[2]
    y_s = _layer_tail(xs, o_moba_s, o_gla_s, rgs, db, tp,
                      cache_mem_k[l].reshape(db, mem_s, D_MODEL), cache_mem_v[l].reshape(db, mem_s, D_MODEL),
                      wts, t, g_final)

    heads = (MOBA_HEADS, MOBA_DH)
    k_s = ks.reshape(db, tp, *heads)[:, :t]
    v_s = vs.reshape(db, tp, *heads)[:, :t]
    return (y_p.reshape(b, s, D_MODEL), y_s.reshape(db, t, D_MODEL),
            k.reshape(1, b, s, *heads), v.reshape(1, b, s, *heads), st_p[None],
            mk_p.reshape(1, b, mem_len, MEM_HEADS, MEM_DH), mv_p.reshape(1, b, mem_len, MEM_HEADS, MEM_DH),
            k_s[None], v_s[None], st_s[None])
```
